```python
import math
import jax, jax.numpy as jnp
from jax import lax
import numpy as np


D_MODEL = 1024
BATCH = 2
SEQ = 8192
DEPTH = 4

GRID_W = 64
CTX_LEN = 256
LN_EPS = 1e-5

HEAD_DIM = 64
N_HEADS_A = D_MODEL // (2 * HEAD_DIM)
V_DIM = 2 * HEAD_DIM
QK_W = N_HEADS_A * 2 * HEAD_DIM
V_W = N_HEADS_A * V_DIM
ATTN_OUT = V_W
ATTN_SCALE = HEAD_DIM ** -0.5
Q_BLOCK = 128
ROPE_BASE = 10000.0
ROPE_AXIS_PAIRS = HEAD_DIM // 4

CHUNK = 128
SGU_GROUPS = 4
SGU_W = D_MODEL // 2
SGU_GC = SGU_W // SGU_GROUPS

AB_IN = 2 * QK_W + V_W + 2 * SGU_W
AB_OUT = ATTN_OUT + SGU_W
AB_SPLITS = [QK_W, 2 * QK_W, 2 * QK_W + V_W, 2 * QK_W + V_W + SGU_W]

POOL_WINDOWS = (2, 4, 8, 16)
N_POOL_GROUPS = len(POOL_WINDOWS)
POOL_W = D_MODEL
POOL_GC = POOL_W // N_POOL_GROUPS

P_HEADS = 8
N_KEYS = 128
N_EXPERTS = N_KEYS * N_KEYS
P_TOPK = 16
D_KEY = 256
D_KEY_HALF = D_KEY // 2
PEER_BLOCK = 128

kernel_name = 'hybrid_diffattn_sgu_pool_peer_trunk'


def layer_norm(h, g, b):
    h32 = h.astype(jnp.float32)
    mu = jnp.mean(h32, axis=-1, keepdims=True)
    var = jnp.mean(jnp.square(h32 - mu), axis=-1, keepdims=True)
    out = (h32 - mu) * lax.rsqrt(var + LN_EPS) * g.astype(jnp.float32) + b.astype(jnp.float32)
    return out.astype(h.dtype)


def modulate(h, shift, scale):
    return h * (1.0 + scale) + shift


def axial_rope_tables(rows):
    row = jnp.repeat(jnp.arange(rows, dtype=jnp.float32), GRID_W)
    col = jnp.tile(jnp.arange(GRID_W, dtype=jnp.float32), rows)
    inv = ROPE_BASE ** (-jnp.arange(ROPE_AXIS_PAIRS, dtype=jnp.float32) / ROPE_AXIS_PAIRS)
    ang = jnp.concatenate([row[:, None] * inv, col[:, None] * inv], axis=-1)
    return jnp.cos(ang), jnp.sin(ang)


def apply_rope(t, cos, sin):
    t32 = t.astype(jnp.float32)
    half = HEAD_DIM // 2
    a, b = t32[..., :half], t32[..., half:]
    cs = cos[None, :, None, None, :]
    sn = sin[None, :, None, None, :]
    return jnp.concatenate([a * cs - b * sn, a * sn + b * cs], axis=-1).astype(t.dtype)


def diff_attn_block(q, k, v, lam):
    s = jnp.einsum('bqhmd,bkhmd->bhmqk', q, k).astype(jnp.float32) * ATTN_SCALE
    p = jax.nn.softmax(s, axis=-1)
    a = (p[:, :, 0] - lam * p[:, :, 1]).astype(v.dtype)
    return jnp.einsum('bhqk,bkhe->bqhe', a, v)


def head_rms(o, g, lam_init):
    o32 = o.astype(jnp.float32)
    o32 = o32 * lax.rsqrt(jnp.mean(o32 * o32, axis=-1, keepdims=True) + LN_EPS)
    return (o32 * g.astype(jnp.float32) * (1.0 - lam_init)).astype(o.dtype)


def spatial_gate(gu, gv, ln_g, ln_b, w_s, b_s):
    u = jax.nn.gelu(gu)
    v = layer_norm(jax.nn.gelu(gv), ln_g, ln_b)
    bsz, t, _ = v.shape
    v = v.reshape(bsz, t // CHUNK, CHUNK, SGU_GROUPS, SGU_GC)
    s = jnp.einsum('gpq,bnqgc->bnpgc', w_s, v) + b_s.T[:, :, None]
    return u * s.reshape(bsz, t, SGU_W)


def mixer_ab(hx, hc, w_in, w_out, lam_vec, norm_g, sgu_ln_g, sgu_ln_b, sgu_w, sgu_b,
             lam_init, cos, sin, ctx_out):
    bsz, s_len, _ = hx.shape
    c_len = hc.shape[1]
    lq1, lk1, lq2, lk2 = lam_vec.astype(jnp.float32)
    lam = jnp.exp(jnp.sum(lq1 * lk1)) - jnp.exp(jnp.sum(lq2 * lk2)) + lam_init
    q_x, k_x, v_x, gu_x, gv_x = jnp.split(hx @ w_in, AB_SPLITS, axis=-1)
    q_x = apply_rope(q_x.reshape(bsz, s_len, N_HEADS_A, 2, HEAD_DIM), cos, sin)
    k_x = apply_rope(k_x.reshape(bsz, s_len, N_HEADS_A, 2, HEAD_DIM), cos, sin)
    v_x = v_x.reshape(bsz, s_len, N_HEADS_A, V_DIM)
    if ctx_out:
        q_c, k_c, v_c, gu_c, gv_c = jnp.split(hc @ w_in, AB_SPLITS, axis=-1)
    else:
        k_c, v_c = jnp.split(hc @ w_in[:, QK_W:2 * QK_W + V_W], [QK_W], axis=-1)
    k_c = k_c.reshape(bsz, c_len, N_HEADS_A, 2, HEAD_DIM)
    v_c = v_c.reshape(bsz, c_len, N_HEADS_A, V_DIM)
    k_all = jnp.concatenate([k_c, k_x], axis=1)
    v_all = jnp.concatenate([v_c, v_x], axis=1)
    n_blk = s_len // Q_BLOCK
    qb = jnp.moveaxis(q_x.reshape(bsz, n_blk, Q_BLOCK, N_HEADS_A, 2, HEAD_DIM), 1, 0)
    o_x = lax.map(lambda qblk: diff_attn_block(qblk, k_all, v_all, lam), qb)
    o_x = jnp.moveaxis(o_x, 0, 1).reshape(bsz, s_len, N_HEADS_A, V_DIM)
    a_x = head_rms(o_x, norm_g, lam_init).reshape(bsz, s_len, ATTN_OUT)
    g_x = spatial_gate(gu_x, gv_x, sgu_ln_g, sgu_ln_b, sgu_w, sgu_b)
    y_x = jnp.concatenate([a_x, g_x], axis=-1) @ w_out
    if not ctx_out:
        return y_x, None
    q_c = q_c.reshape(bsz, c_len, N_HEADS_A, 2, HEAD_DIM)
    o_c = diff_attn_block(q_c, k_c, v_c, lam)
    a_c = head_rms(o_c, norm_g, lam_init).reshape(bsz, c_len, ATTN_OUT)
    g_c = spatial_gate(gu_c, gv_c, sgu_ln_g, sgu_ln_b, sgu_w, sgu_b)
    y_c = jnp.concatenate([a_c, g_c], axis=-1) @ w_out
    return y_x, y_c


def multiscale_pool(h):
    bsz, t, _ = h.shape
    h32 = h.astype(jnp.float32)
    cs = jnp.concatenate([jnp.zeros((bsz, 1, POOL_W), jnp.float32), jnp.cumsum(h32, axis=1)], axis=1)
    pos = jnp.arange(t)
    outs = []
    for g, w in enumerate(POOL_WINDOWS):
        lo = jnp.clip(pos - w // 2, 0, t)
        hi = jnp.clip(pos + (w - w // 2), 0, t)
        seg = cs[:, :, g * POOL_GC:(g + 1) * POOL_GC]
        cnt = (hi - lo).astype(jnp.float32)[None, :, None]
        outs.append((seg[:, hi] - seg[:, lo]) / cnt)
    pooled = jnp.concatenate(outs, axis=-1)
    return (pooled - h32).astype(h.dtype)


def mixer_pool(h, w_in, w_grp, scale, w_out):
    m = multiscale_pool(h @ w_in)
    bsz, t, _ = m.shape
    m = jnp.einsum('btgc,gce->btge', m.reshape(bsz, t, N_POOL_GROUPS, POOL_GC), w_grp)
    return (m.reshape(bsz, t, POOL_W) * scale) @ w_out


def peer(h, wq, keys, u_tab, v_tab):
    n, d = h.shape
    hb = h.reshape(n // PEER_BLOCK, PEER_BLOCK, d)

    def block(hblk):
        q = (hblk @ wq).reshape(PEER_BLOCK, P_HEADS, 2, D_KEY_HALF)
        s = jnp.einsum('nhpd,hpkd->nhpk', q, keys).astype(jnp.float32)
        top_s, top_i = lax.top_k(s, P_TOPK)
        cand_s = (top_s[:, :, 0, :, None] + top_s[:, :, 1, None, :]).reshape(PEER_BLOCK, P_HEADS, P_TOPK * P_TOPK)
        cand_i = (top_i[:, :, 0, :, None] * N_KEYS + top_i[:, :, 1, None, :]).reshape(PEER_BLOCK, P_HEADS, P_TOPK * P_TOPK)
        best_s, best_pos = lax.top_k(cand_s, P_TOPK)
        idx = jnp.take_along_axis(cand_i, best_pos, axis=-1)
        gate = jax.nn.softmax(best_s, axis=-1)
        u = u_tab[idx]
        v = v_tab[idx]
        act = jax.nn.gelu(jnp.einsum('nd,nhkd->nhk', hblk, u))
        return jnp.einsum('nhk,nhkd->nd', (gate * act).astype(v.dtype), v)

    return lax.map(block, hb).reshape(n, d)


def setup_inputs(seed: int = 0) -> dict:
    key = jax.random.key(seed)
    ks = jax.random.split(key, 24)
    n_even = (DEPTH + 1) // 2
    n_odd = DEPTH // 2
    beta = (8.0 * DEPTH) ** -0.25

    def nrm(k, shape, s):
        return jax.random.normal(k, shape, jnp.float32) * s

    return {
        'x': nrm(ks[0], (BATCH, SEQ, D_MODEL), 1.0),
        'c': nrm(ks[1], (BATCH, D_MODEL), 1.0),
        'ctx': nrm(ks[2], (BATCH, CTX_LEN, D_MODEL), 1.0),
        'c_ctx': nrm(ks[3], (D_MODEL,), 1.0),
        'ada_w': nrm(ks[4], (DEPTH, D_MODEL, 6 * D_MODEL), 0.5 * D_MODEL ** -0.5),
        'ada_b': nrm(ks[5], (DEPTH, 6 * D_MODEL), 0.02),
        'ln_g': 1.0 + nrm(ks[6], (DEPTH, 2, D_MODEL), 0.05),
        'ln_b': nrm(ks[7], (DEPTH, 2, D_MODEL), 0.02),
        'ab_w_in': nrm(ks[8], (n_even, D_MODEL, AB_IN), D_MODEL ** -0.5),
        'ab_w_out': nrm(ks[9], (n_even, AB_OUT, D_MODEL), beta * AB_OUT ** -0.5),
        'diff_lam': nrm(ks[10], (n_even, 4, HEAD_DIM), 0.1),
        'diff_norm_g': 1.0 + nrm(ks[11], (n_even, V_DIM), 0.05),
        'sgu_ln_g': 1.0 + nrm(ks[12], (n_even, SGU_W), 0.05),
        'sgu_ln_b': nrm(ks[13], (n_even, SGU_W), 0.02),
        'sgu_w': nrm(ks[14], (n_even, SGU_GROUPS, CHUNK, CHUNK), CHUNK ** -0.5),
        'sgu_b': 1.0 + nrm(ks[15], (n_even, SGU_GROUPS, CHUNK), 0.1),
        'pool_w_in': nrm(ks[16], (n_odd, D_MODEL, POOL_W), D_MODEL ** -0.5),
        'pool_w_grp': nrm(ks[17], (n_odd, N_POOL_GROUPS, POOL_GC, POOL_GC), POOL_GC ** -0.5),
        'pool_scale': 1.0 + nrm(ks[18], (n_odd, POOL_W), 0.1),
        'pool_w_out': nrm(ks[19], (n_odd, POOL_W, D_MODEL), beta * POOL_W ** -0.5),
        'peer_wq': nrm(ks[20], (DEPTH, D_MODEL, P_HEADS * D_KEY), D_MODEL ** -0.5),
        'peer_keys': nrm(ks[21], (DEPTH, P_HEADS, 2, N_KEYS, D_KEY_HALF), D_KEY_HALF ** -0.5),
        'peer_u': nrm(ks[22], (DEPTH, N_EXPERTS, D_MODEL), D_MODEL ** -0.5),
        'peer_v': nrm(ks[23], (DEPTH, N_EXPERTS, D_MODEL), beta),
    }


def reference(x, c, ctx, c_ctx, ada_w, ada_b, ln_g, ln_b, ab_w_in, ab_w_out, diff_lam,
              diff_norm_g, sgu_ln_g, sgu_ln_b, sgu_w, sgu_b, pool_w_in, pool_w_grp,
              pool_scale, pool_w_out, peer_wq, peer_keys, peer_u, peer_v):
    bsz, s_len, d = x.shape
    c_len = ctx.shape[1]
    rows = s_len // GRID_W
    cos, sin = axial_rope_tables(rows)
    alpha = (2.0 * DEPTH) ** 0.25
    last_ctx_read = 2 * ((DEPTH - 1) // 2)
    silu_c = jax.nn.silu(c)
    silu_cc = jax.nn.silu(c_ctx)
    xs, cs = x, ctx
    for i in range(DEPTH):
        j = i // 2
        even = i % 2 == 0
        ctx_out = i < last_ctx_read
        sh1, sc1, g1, sh2, sc2, g2 = jnp.split((silu_c @ ada_w[i] + ada_b[i])[:, None, :], 6, axis=-1)
        hx = modulate(xs, sh1, sc1)
        if ctx_out or even:
            csh1, csc1, cg1, csh2, csc2, cg2 = jnp.split(silu_cc @ ada_w[i] + ada_b[i], 6, axis=-1)
            hc = modulate(cs, csh1, csc1)
        if even:
            lam_init = 0.8 - 0.6 * math.exp(-0.3 * i)
            yx, yc = mixer_ab(hx, hc, ab_w_in[j], ab_w_out[j], diff_lam[j], diff_norm_g[j],
                              sgu_ln_g[j], sgu_ln_b[j], sgu_w[j], sgu_b[j], lam_init, cos, sin, ctx_out)
        else:
            yx = mixer_pool(hx, pool_w_in[j], pool_w_grp[j], pool_scale[j], pool_w_out[j])
            yc = mixer_pool(hc, pool_w_in[j], pool_w_grp[j], pool_scale[j], pool_w_out[j]) if ctx_out else None
        xs = layer_norm(alpha * xs + g1 * yx, ln_g[i, 0], ln_b[i, 0])
        hx = modulate(xs, sh2, sc2)
        if ctx_out:
            cs = layer_norm(alpha * cs + cg1 * yc, ln_g[i, 0], ln_b[i, 0])
            hc = modulate(cs, csh2, csc2)
            f = peer(jnp.concatenate([hc, hx], axis=1).reshape(-1, d),
                     peer_wq[i], peer_keys[i], peer_u[i], peer_v[i]).reshape(bsz, c_len + s_len, d)
            fc, fx = f[:, :c_len], f[:, c_len:]
            cs = layer_norm(alpha * cs + cg2 * fc, ln_g[i, 1], ln_b[i, 1])
        else:
            fx = peer(hx.reshape(-1, d), peer_wq[i], peer_keys[i], peer_u[i], peer_v[i]).reshape(bsz, s_len, d)
        xs = layer_norm(alpha * xs + g2 * fx, ln_g[i, 1], ln_b[i, 1])
    return xs
```

```python
import functools
import math

import jax
import jax.numpy as jnp
from jax import lax
from jax.experimental import pallas as pl
from jax.experimental.pallas import tpu as pltpu

F32 = jnp.float32
BF16 = jnp.bfloat16

GRID_W = 64
LN_EPS = 1e-5
HEAD_DIM = 64
V_DIM = 2 * HEAD_DIM
ATTN_SCALE = HEAD_DIM ** -0.5
ROPE_BASE = 10000.0
ROPE_AXIS_PAIRS = HEAD_DIM // 4
CHUNK = 128
SGU_GROUPS = 4
POOL_WINDOWS = (2, 4, 8, 16)
P_HEADS = 8
N_KEYS = 128
P_TOPK = 16

LANE = 128
VMEM_LIMIT = 56 * 1024 * 1024
TOKEN_TILE = 512


def _params(*sem):
    return pltpu.CompilerParams(dimension_semantics=sem, vmem_limit_bytes=VMEM_LIMIT)


def _mm_kernel(*refs, n_pairs, has_bias):
    o_ref = refs[-1]
    acc = None
    for i in range(n_pairs):
        part = jnp.dot(refs[2 * i][...], refs[2 * i + 1][...], preferred_element_type=F32)
        acc = part if acc is None else acc + part
    if has_bias:
        acc = acc + refs[2 * n_pairs][...]
    o_ref[...] = acc.astype(o_ref.dtype)


def matmul_fm(pairs, out_dtype, bias=None, tn=TOKEN_TILE, tm=None):
    m = pairs[0][0].shape[0]
    n = pairs[0][1].shape[1]
    tn = min(tn, n)
    tm = m if tm is None else tm
    args, specs = [], []
    for w, x in pairs:
        k = w.shape[1]
        args += [w, x]
        specs += [pl.BlockSpec((tm, k), lambda j, i: (i, 0)),
                  pl.BlockSpec((k, tn), lambda j, i: (0, j))]
    if bias is not None:
        args.append(bias)
        specs.append(pl.BlockSpec((tm, 1), lambda j, i: (i, 0)))
    return pl.pallas_call(
        functools.partial(_mm_kernel, n_pairs=len(pairs), has_bias=bias is not None),
        grid=(n // tn, m // tm),
        in_specs=specs,
        out_specs=pl.BlockSpec((tm, tn), lambda j, i: (i, j)),
        out_shape=jax.ShapeDtypeStruct((m, n), out_dtype),
        compiler_params=_params("parallel", "parallel"),
        name="matmul_fm",
    )(*args)


def _attn_kernel(lam_ref, q_ref, k_ref, v_ref, g_ref, o_ref, qp_ref, m_ref, l_ref, acc_ref,
                 *, tq, n_chunks, out_scale):
    q = q_ref[...]
    row = lax.broadcasted_iota(jnp.int32, q.shape, 0)
    zero = jnp.zeros_like(q)
    qp_ref[:, :tq] = jnp.where(row < HEAD_DIM, q, zero)
    qp_ref[:, tq:] = jnp.where(row >= HEAD_DIM, q, zero)
    m_ref[...] = jnp.full(m_ref.shape, -jnp.inf, F32)
    l_ref[...] = jnp.zeros(l_ref.shape, F32)
    acc_ref[...] = jnp.zeros(acc_ref.shape, F32)

    def body(c, carry):
        s = jnp.dot(k_ref[c], qp_ref[...], preferred_element_type=F32)
        m_prev = m_ref[...]
        m_new = jnp.maximum(m_prev, jnp.max(s, axis=0, keepdims=True))
        alpha = jnp.exp(m_prev - m_new)
        p = jnp.exp(s - m_new)
        l_ref[...] = alpha * l_ref[...] + jnp.sum(p, axis=0, keepdims=True)
        acc_ref[...] = acc_ref[...] * alpha + jnp.dot(
            v_ref[c], p.astype(BF16), preferred_element_type=F32)
        m_ref[...] = m_new
        return carry

    lax.fori_loop(0, n_chunks, body, 0)

    lam = lam_ref[0, 0]
    inv = 1.0 / l_ref[...]
    acc = acc_ref[...]
    o = acc[:, :tq] * inv[:, :tq] - lam * (acc[:, tq:] * inv[:, tq:])
    ms = jnp.mean(o * o, axis=0, keepdims=True)
    o = o * lax.rsqrt(ms + LN_EPS) * g_ref[...] * out_scale
    o_ref[...] = o.astype(o_ref.dtype)


def diff_attention(q_fm, k_chunks, v_chunks, lam, norm_g, lam_init, n_batch, tq):
    n_heads = q_fm.shape[0] // V_DIM
    sq = q_fm.shape[1] // n_batch
    nq = sq // tq
    _, _, nc, tk, _ = k_chunks.shape
    return pl.pallas_call(
        functools.partial(_attn_kernel, tq=tq, n_chunks=nc, out_scale=1.0 - lam_init),
        grid=(n_batch, n_heads, nq),
        in_specs=[
            pl.BlockSpec(memory_space=pltpu.SMEM),
            pl.BlockSpec((V_DIM, tq), lambda b, h, i: (h, b * nq + i)),
            pl.BlockSpec((None, None, nc, tk, V_DIM), lambda b, h, i: (b, h, 0, 0, 0)),
            pl.BlockSpec((None, None, nc, V_DIM, tk), lambda b, h, i: (b, h, 0, 0, 0)),
            pl.BlockSpec((V_DIM, 1), lambda b, h, i: (0, 0)),
        ],
        out_specs=pl.BlockSpec((V_DIM, tq), lambda b, h, i: (h, b * nq + i)),
        out_shape=jax.ShapeDtypeStruct(q_fm.shape, BF16),
        scratch_shapes=[
            pltpu.VMEM((V_DIM, 2 * tq), BF16),
            pltpu.VMEM((1, 2 * tq), F32),
            pltpu.VMEM((1, 2 * tq), F32),
            pltpu.VMEM((V_DIM, 2 * tq), F32),
        ],
        compiler_params=_params("parallel", "parallel", "parallel"),
        name="diff_attention",
    )(lam, q_fm, k_chunks, v_chunks, norm_g)


def _sgu_kernel(gu_ref, gv_ref, lng_ref, lnb_ref, wst_ref, bs_ref, o_ref, *, n_sub):
    v = jax.nn.gelu(gv_ref[...])
    mu = jnp.mean(v, axis=0, keepdims=True)
    var = jnp.mean(jnp.square(v - mu), axis=0, keepdims=True)
    v = (v - mu) * lax.rsqrt(var + LN_EPS) * lng_ref[...] + lnb_ref[...]
    vb = v.astype(BF16)
    gc = vb.shape[0] // SGU_GROUPS
    for g in range(SGU_GROUPS):
        for j in range(n_sub):
            blk = vb[g * gc:(g + 1) * gc, j * CHUNK:(j + 1) * CHUNK]
            s = jnp.dot(blk, wst_ref[g], preferred_element_type=F32) + bs_ref[g]
            u = jax.nn.gelu(gu_ref[g * gc:(g + 1) * gc, j * CHUNK:(j + 1) * CHUNK])
            o_ref[g * gc:(g + 1) * gc, j * CHUNK:(j + 1) * CHUNK] = (u * s).astype(o_ref.dtype)


def spatial_gate_fm(gu, gv, ln_g, ln_b, ws_t, bs, tn=TOKEN_TILE):
    w, n = gu.shape
    tn = min(tn, n)
    return pl.pallas_call(
        functools.partial(_sgu_kernel, n_sub=tn // CHUNK),
        grid=(n // tn,),
        in_specs=[
            pl.BlockSpec((w, tn), lambda j: (0, j)),
            pl.BlockSpec((w, tn), lambda j: (0, j)),
            pl.BlockSpec((w, 1), lambda j: (0, 0)),
            pl.BlockSpec((w, 1), lambda j: (0, 0)),
            pl.BlockSpec((SGU_GROUPS, CHUNK, CHUNK), lambda j: (0, 0, 0)),
            pl.BlockSpec((SGU_GROUPS, 1, CHUNK), lambda j: (0, 0, 0)),
        ],
        out_specs=pl.BlockSpec((w, tn), lambda j: (0, j)),
        out_shape=jax.ShapeDtypeStruct((w, n), BF16),
        compiler_params=_params("parallel"),
        name="spatial_gate",
    )(gu, gv, ln_g, ln_b, ws_t, bs)


POOL_TILE = 256
POOL_HALO = 128


def _pool_kernel(left_ref, mid_ref, right_ref, wg_ref, scale_ref, wout_ref, o_ref, *, seq_len):
    j = pl.program_id(0)
    t0 = j * POOL_TILE
    seq_start = (t0 // seq_len) * seq_len
    win = POOL_TILE + 2 * POOL_HALO
    h_all = jnp.concatenate([left_ref[...], mid_ref[...], right_ref[...]], axis=1)
    h_hi = h_all.astype(BF16)
    h_lo = (h_all - h_hi.astype(F32)).astype(BF16)
    tau = t0 - POOL_HALO + lax.broadcasted_iota(jnp.int32, (win, POOL_TILE), 0)
    t = t0 + lax.broadcasted_iota(jnp.int32, (win, POOL_TILE), 1)
    gc = mid_ref.shape[0] // len(POOL_WINDOWS)
    outs = []
    for g, w in enumerate(POOL_WINDOWS):
        lo = jnp.maximum(t - w // 2, seq_start)
        hi = jnp.minimum(t + (w - w // 2), seq_start + seq_len)
        band = jnp.where((tau >= lo) & (tau < hi), 1.0, 0.0).astype(BF16)
        cnt = (hi - lo)[0:1, :].astype(F32)
        sl = slice(g * gc, (g + 1) * gc)
        ssum = (jnp.dot(h_hi[sl], band, preferred_element_type=F32)
                + jnp.dot(h_lo[sl], band, preferred_element_type=F32))
        m = ssum / cnt - mid_ref[sl, :]
        m = jnp.dot(wg_ref[g], m.astype(BF16), preferred_element_type=F32)
        outs.append(m)
    m_all = (jnp.concatenate(outs, axis=0) * scale_ref[...]).astype(BF16)
    o_ref[...] = jnp.dot(wout_ref[...], m_all, preferred_element_type=F32)


def pool_mixer_fm(h, wg_t, scale, wout_t, seq_len):
    c, n = h.shape
    nt = n // POOL_TILE
    r = POOL_TILE // POOL_HALO
    nh = n // POOL_HALO
    return pl.pallas_call(
        functools.partial(_pool_kernel, seq_len=seq_len),
        grid=(nt,),
        in_specs=[
            pl.BlockSpec((c, POOL_HALO), lambda j: (0, jnp.maximum(j * r - 1, 0))),
            pl.BlockSpec((c, POOL_TILE), lambda j: (0, j)),
            pl.BlockSpec((c, POOL_HALO), lambda j: (0, jnp.minimum(j * r + r, nh - 1))),
            pl.BlockSpec(wg_t.shape, lambda j: (0, 0, 0)),
            pl.BlockSpec((c, 1), lambda j: (0, 0)),
            pl.BlockSpec(wout_t.shape, lambda j: (0, 0)),
        ],
        out_specs=pl.BlockSpec((wout_t.shape[0], POOL_TILE), lambda j: (0, j)),
        out_shape=jax.ShapeDtypeStruct((wout_t.shape[0], n), F32),
        compiler_params=_params("parallel"),
        name="pool_mixer",
    )(h, h, h, wg_t, scale, wout_t)


def _peer_scores_kernel(h_ref, wq_ref, keys_ref, s_ref):
    q = jnp.dot(wq_ref[...], h_ref[...], preferred_element_type=F32).astype(BF16)
    for hp in range(2 * P_HEADS):
        h, p = divmod(hp, 2)
        s = jnp.dot(keys_ref[hp], q[hp * N_KEYS:(hp + 1) * N_KEYS, :], preferred_element_type=F32)
        s_ref[p, pl.ds(h, N_KEYS, stride=P_HEADS), :] = s


def peer_scores(h_bf, wq_t, keys_bf, tn=LANE):
    d, n = h_bf.shape
    tn = min(tn, n)
    return pl.pallas_call(
        _peer_scores_kernel,
        grid=(n // tn,),
        in_specs=[
            pl.BlockSpec((d, tn), lambda j: (0, j)),
            pl.BlockSpec(wq_t.shape, lambda j: (0, 0)),
            pl.BlockSpec(keys_bf.shape, lambda j: (0, 0, 0)),
        ],
        out_specs=pl.BlockSpec((2, N_KEYS * P_HEADS, tn), lambda j: (0, 0, j)),
        out_shape=jax.ShapeDtypeStruct((2, N_KEYS * P_HEADS, n), F32),
        compiler_params=_params("parallel"),
        name="peer_scores",
    )(h_bf, wq_t, keys_bf)


def _sort16_desc(xs):
    xs = list(xs)
    n = len(xs)
    k = 2
    while k <= n:
        j = k // 2
        while j >= 1:
            for i in range(n):
                l = i ^ j
                if l > i:
                    hi, lo = jnp.maximum(xs[i], xs[l]), jnp.minimum(xs[i], xs[l])
                    if (i & k) == 0:
                        xs[i], xs[l] = hi, lo
                    else:
                        xs[i], xs[l] = lo, hi
            j //= 2
        k *= 2
    return xs


def _bitonic_merge_desc(xs):
    xs = list(xs)
    n = len(xs)
    j = n // 2
    while j >= 1:
        for i in range(n):
            l = i ^ j
            if l > i:
                xs[i], xs[l] = jnp.maximum(xs[i], xs[l]), jnp.minimum(xs[i], xs[l])
        j //= 2
    return xs


def _top16_of_two(a, b):
    n = len(a)
    return _bitonic_merge_desc([jnp.maximum(a[i], b[n - 1 - i]) for i in range(n)])


def _top16_desc(vals):
    groups = [_sort16_desc(vals[i:i + P_TOPK]) for i in range(0, len(vals), P_TOPK)]
    while len(groups) > 1:
        groups = [_top16_of_two(groups[i], groups[i + 1]) for i in range(0, len(groups), 2)]
    return groups[0]


def _peer_route_kernel(s_ref, lrow_ref, e1_ref, rank2_ref, e2_ref):
    s1 = [s_ref[0, k] for k in range(N_KEYS)]
    s2 = [s_ref[1, k] for k in range(N_KEYS)]
    a = _top16_desc(s1)
    b = _top16_desc(s2)
    cand = [[a[r1] + b[r2] for r2 in range(P_TOPK)] for r1 in range(P_TOPK)]
    rows = [list(r) for r in cand]
    while len(rows) > 1:
        rows = [_top16_of_two(rows[i], rows[i + 1]) for i in range(0, len(rows), 2)]
    best = rows[0]
    thr = best[P_TOPK - 1]
    z = None
    for r in range(P_TOPK):
        e = jnp.exp(best[r] - best[0])
        z = e if z is None else z + e
    inv_z = 1.0 / z
    counts = []
    for r1 in range(P_TOPK):
        cnt = jnp.zeros_like(thr)
        for r2 in range(P_TOPK):
            cnt = cnt + jnp.where(cand[r1][r2] >= thr, 1.0, 0.0)
        counts.append(cnt)
    def per_key(k, carry):
        s1k = s_ref[0, k]
        s2k = s_ref[1, k]
        lrow = jnp.zeros_like(thr)
        rank = jnp.zeros_like(thr)
        for r in range(P_TOPK):
            lrow = jnp.where(s1k == a[r], counts[r], lrow)
            rank = jnp.where(b[r] > s2k, float(r + 1), rank)
        lrow_ref[k] = lrow
        e1_ref[k] = jnp.exp(s1k - a[0])
        rank2_ref[pl.ds(k, P_HEADS, stride=N_KEYS), :] = rank
        e2_ref[pl.ds(k, P_HEADS, stride=N_KEYS), :] = jnp.exp(s2k - b[0]) * inv_z
        return carry

    lax.fori_loop(0, N_KEYS, per_key, 0)


def peer_route(scores, tn=LANE):
    _, rows, n = scores.shape
    s4 = scores.reshape(2, N_KEYS, P_HEADS, n)
    key_major = jax.ShapeDtypeStruct((N_KEYS, P_HEADS, n), F32)
    head_major = jax.ShapeDtypeStruct((rows, n), F32)
    km_spec = pl.BlockSpec((N_KEYS, P_HEADS, tn), lambda j: (0, 0, j))
    hm_spec = pl.BlockSpec((rows, tn), lambda j: (0, j))
    lrow, e1, rank2, e2 = pl.pallas_call(
        _peer_route_kernel,
        grid=(n // tn,),
        in_specs=[pl.BlockSpec((2, N_KEYS, P_HEADS, tn), lambda j: (0, 0, 0, j))],
        out_specs=[km_spec, km_spec, hm_spec, hm_spec],
        out_shape=[key_major, key_major, head_major, head_major],
        compiler_params=_params("parallel"),
        name="peer_route",
    )(s4)
    return lrow.reshape(rows, n), e1.reshape(rows, n), rank2, e2


def _peer_dense_kernel(h_ref, u_ref, vt_ref, lrow_ref, e1_ref, rank2_ref, e2_ref, o_ref,
                       r2_s, e2_s, w_s, *, i1_tile):
    e = pl.program_id(1)

    @pl.when(e == 0)
    def _():
        o_ref[...] = jnp.zeros(o_ref.shape, F32)
        for h in range(P_HEADS):
            r2_s[h] = rank2_ref[h * N_KEYS:(h + 1) * N_KEYS, :].astype(BF16)
            e2_s[h] = e2_ref[h * N_KEYS:(h + 1) * N_KEYS, :].astype(BF16)

    act = jax.nn.gelu(jnp.dot(u_ref[...], h_ref[...], preferred_element_type=F32)).astype(BF16)
    tn = act.shape[1]
    for il in range(i1_tile):
        lr = lrow_ref[il * P_HEADS:(il + 1) * P_HEADS, :]
        e1 = e1_ref[il * P_HEADS:(il + 1) * P_HEADS, :]
        gate = None
        for h in range(P_HEADS):
            lb = jnp.broadcast_to(lr[h:h + 1, :], (N_KEYS, tn)).astype(BF16)
            eb = jnp.broadcast_to(e1[h:h + 1, :], (N_KEYS, tn)).astype(BF16)
            term = jnp.where(r2_s[h] < lb, e2_s[h], jnp.zeros((), BF16)) * eb
            gate = term if gate is None else gate + term
        w_s[il * N_KEYS:(il + 1) * N_KEYS, :] = gate * act[il * N_KEYS:(il + 1) * N_KEYS, :]
    o_ref[...] += jnp.dot(vt_ref[...], w_s[...], preferred_element_type=F32)


def peer_dense(h_bf, u_bf, vt_bf, lrow, e1, rank2, e2, tn=TOKEN_TILE, i1_tile=16):
    d, n = h_bf.shape
    n_exp = u_bf.shape[0]
    tn = min(tn, n)
    et = i1_tile * N_KEYS
    rows = N_KEYS * P_HEADS
    return pl.pallas_call(
        functools.partial(_peer_dense_kernel, i1_tile=i1_tile),
        grid=(n // tn, n_exp // et),
        in_specs=[
            pl.BlockSpec((d, tn), lambda j, e: (0, j)),
            pl.BlockSpec((et, d), lambda j, e: (e, 0)),
            pl.BlockSpec((d, et), lambda j, e: (0, e)),
            pl.BlockSpec((i1_tile * P_HEADS, tn), lambda j, e: (e, j)),
            pl.BlockSpec((i1_tile * P_HEADS, tn), lambda j, e: (e, j)),
            pl.BlockSpec((rows, tn), lambda j, e: (0, j)),
            pl.BlockSpec((rows, tn), lambda j, e: (0, j)),
        ],
        out_specs=pl.BlockSpec((d, tn), lambda j, e: (0, j)),
        out_shape=jax.ShapeDtypeStruct((d, n), F32),
        scratch_shapes=[
            pltpu.VMEM((P_HEADS, N_KEYS, tn), BF16),
            pltpu.VMEM((P_HEADS, N_KEYS, tn), BF16),
            pltpu.VMEM((et, tn), BF16),
        ],
        compiler_params=_params("parallel", "arbitrary"),
        name="peer_dense",
    )(h_bf, u_bf, vt_bf, lrow, e1, rank2, e2)


def peer_fm(h_bf, wq_t, keys_bf, u_bf, vt_bf):
    scores = peer_scores(h_bf, wq_t, keys_bf)
    lrow, e1, rank2, e2 = peer_route(scores)
    return peer_dense(h_bf, u_bf, vt_bf, lrow, e1, rank2, e2)


def _layer_norm_fm(h, g, b):
    mu = jnp.mean(h, axis=0, keepdims=True)
    var = jnp.mean(jnp.square(h - mu), axis=0, keepdims=True)
    return (h - mu) * lax.rsqrt(var + LN_EPS) * g[:, None] + b[:, None]


def _modulate_fm(h, shift, scale, n_batch):
    d, n = h.shape
    nb = shift.shape[1]
    h3 = h.reshape(d, nb, n // nb)
    return (h3 * (1.0 + scale[:, :, None]) + shift[:, :, None]).reshape(d, n)


def _gated_residual_fm(xs, gate, y, alpha):
    d, n = xs.shape
    nb = gate.shape[1]
    return (alpha * xs.reshape(d, nb, n // nb) + gate[:, :, None] * y.reshape(d, nb, n // nb)).reshape(d, n)


def _rope_tables_fm(rows):
    row = jnp.repeat(jnp.arange(rows, dtype=F32), GRID_W)
    col = jnp.tile(jnp.arange(GRID_W, dtype=F32), rows)
    inv = ROPE_BASE ** (-jnp.arange(ROPE_AXIS_PAIRS, dtype=F32) / ROPE_AXIS_PAIRS)
    ang = jnp.concatenate([inv[:, None] * row[None, :], inv[:, None] * col[None, :]], axis=0)
    return jnp.cos(ang), jnp.sin(ang)


def _rope_fm(t, cos, sin, n_batch):
    rows, n = t.shape
    half = HEAD_DIM // 2
    t5 = t.reshape(rows // HEAD_DIM, 2, half, n_batch, n // n_batch)
    a, b = t5[:, 0], t5[:, 1]
    cs, sn = cos[None, :, None, :], sin[None, :, None, :]
    out = jnp.stack([a * cs - b * sn, a * sn + b * cs], axis=1)
    return out.reshape(rows, n)


def _kv_chunks(kc, kx, vc, vx, n_batch, tk):
    rows = kx.shape[0]
    nh = rows // V_DIM

    def per_batch(c_part, x_part):
        parts = [x_part.reshape(nh, V_DIM, n_batch, -1)]
        if c_part is not None:
            parts = [c_part.reshape(nh, V_DIM, n_batch, -1)] + parts
        return jnp.concatenate(parts, axis=-1)

    k = per_batch(kc, kx)
    v = per_batch(vc, vx)
    l = k.shape[-1]
    k = jnp.transpose(k, (2, 0, 3, 1)).reshape(n_batch, nh, l // tk, tk, V_DIM)
    v = jnp.transpose(v.reshape(nh, V_DIM, n_batch, l // tk, tk), (2, 0, 3, 1, 4))
    return k.astype(BF16), v.astype(BF16)


def _pick_tk(l):
    for tk in (768, 512, 256, 128):
        if l % tk == 0:
            return tk
    raise ValueError(f"unsupported key length {l}")


def kernel(x, c, ctx, c_ctx, ada_w, ada_b, ln_g, ln_b, ab_w_in, ab_w_out, diff_lam, diff_norm_g,
           sgu_ln_g, sgu_ln_b, sgu_w, sgu_b, pool_w_in, pool_w_grp, pool_scale, pool_w_out,
           peer_wq, peer_keys, peer_u, peer_v):
    bsz, s_len, d = x.shape
    c_len = ctx.shape[1]
    depth = ada_w.shape[0]
    qk_w = ab_w_in.shape[2] // 4
    sgu_half = (ab_w_in.shape[2] - 3 * qk_w) // 2
    alpha = (2.0 * depth) ** 0.25
    last_ctx_read = 2 * ((depth - 1) // 2)
    cos, sin = _rope_tables_fm(s_len // GRID_W)

    xs = x.reshape(bsz * s_len, d).T
    cs = ctx.reshape(bsz * c_len, d).T

    cond = jnp.zeros((d, LANE), F32)
    cond = cond.at[:, :bsz].set(jax.nn.silu(c).T).at[:, bsz].set(jax.nn.silu(c_ctx))
    cond = cond.astype(BF16)

    for i in range(depth):
        j = i // 2
        even = i % 2 == 0
        ctx_out = i < last_ctx_read
        ada = matmul_fm([(ada_w[i].T.astype(BF16), cond)], F32, bias=ada_b[i][:, None])
        sh1, sc1, g1, sh2, sc2, g2 = [ada[k * d:(k + 1) * d, :bsz] for k in range(6)]
        csh1, csc1, cg1, csh2, csc2, cg2 = [ada[k * d:(k + 1) * d, bsz:bsz + 1] for k in range(6)]
        hx = _modulate_fm(xs, sh1, sc1, bsz)
        hc = _modulate_fm(cs, csh1, csc1, bsz) if (ctx_out or even) else None

        if even:
            lam_init = 0.8 - 0.6 * math.exp(-0.3 * i)
            lq1, lk1, lq2, lk2 = diff_lam[j].astype(F32)
            lam = (jnp.exp(jnp.sum(lq1 * lk1)) - jnp.exp(jnp.sum(lq2 * lk2)) + lam_init).reshape(1, 1)
            w_in_t = ab_w_in[j].T.astype(BF16)
            w_out_t = ab_w_out[j].T.astype(BF16)
            norm_g = jnp.tile(diff_norm_g[j], 1)[:, None]
            ws_t = jnp.transpose(sgu_w[j], (0, 2, 1)).astype(BF16)
            bs = sgu_b[j][:, None, :]
            px = matmul_fm([(w_in_t, hx.astype(BF16))], F32)
            q_x = (_rope_fm(px[:qk_w], cos, sin, bsz) * ATTN_SCALE).astype(BF16)
            k_x = _rope_fm(px[qk_w:2 * qk_w], cos, sin, bsz)
            v_x = px[2 * qk_w:3 * qk_w]
            if ctx_out:
                pc = matmul_fm([(w_in_t, hc.astype(BF16))], F32)
            else:
                pc = matmul_fm([(w_in_t[qk_w:3 * qk_w], hc.astype(BF16))], F32)
                pc = jnp.concatenate([jnp.zeros((qk_w, pc.shape[1]), F32), pc], axis=0)
            k_c, v_c = pc[qk_w:2 * qk_w], pc[2 * qk_w:3 * qk_w]
            l_all = c_len + s_len
            k_all, v_all = _kv_chunks(k_c, k_x, v_c, v_x, bsz, _pick_tk(l_all))
            a_x = diff_attention(q_x, k_all, v_all, lam, norm_g, lam_init, bsz, tq=min(512, s_len))
            g_x = spatial_gate_fm(px[3 * qk_w:3 * qk_w + sgu_half], px[3 * qk_w + sgu_half:],
                                  sgu_ln_g[j][:, None], sgu_ln_b[j][:, None], ws_t, bs)
            yx = matmul_fm([(w_out_t[:, :qk_w], a_x), (w_out_t[:, qk_w:], g_x)], F32)
            if ctx_out:
                q_c = (pc[:qk_w] * ATTN_SCALE).astype(BF16)
                kc_only, vc_only = _kv_chunks(None, k_c, None, v_c, bsz, _pick_tk(c_len))
                a_c = diff_attention(q_c, kc_only, vc_only, lam, norm_g, lam_init, bsz, tq=c_len)
                g_c = spatial_gate_fm(pc[3 * qk_w:3 * qk_w + sgu_half], pc[3 * qk_w + sgu_half:],
                                      sgu_ln_g[j][:, None], sgu_ln_b[j][:, None], ws_t, bs)
                yc = matmul_fm([(w_out_t[:, :qk_w], a_c), (w_out_t[:, qk_w:], g_c)], F32)
        else:
            p_in_t = pool_w_in[j].T.astype(BF16)
            wg_t = jnp.transpose(pool_w_grp[j], (0, 2, 1)).astype(BF16)
            p_out_t = pool_w_out[j].T.astype(BF16)
            scale = pool_scale[j][:, None]
            yx = pool_mixer_fm(matmul_fm([(p_in_t, hx.astype(BF16))], F32), wg_t, scale, p_out_t, s_len)
            if ctx_out:
                yc = pool_mixer_fm(matmul_fm([(p_in_t, hc.astype(BF16))], F32), wg_t, scale, p_out_t, c_len)

        xs = _layer_norm_fm(_gated_residual_fm(xs, g1, yx, alpha), ln_g[i, 0], ln_b[i, 0])
        hx = _modulate_fm(xs, sh2, sc2, bsz)

        wq_t = peer_wq[i].T.astype(BF16)
        keys_bf = peer_keys[i].reshape(2 * P_HEADS, N_KEYS, -1).astype(BF16)
        u_bf = peer_u[i].astype(BF16)
        vt_bf = peer_v[i].T.astype(BF16)
        if ctx_out:
            cs = _layer_norm_fm(_gated_residual_fm(cs, cg1, yc, alpha), ln_g[i, 0], ln_b[i, 0])
            hc = _modulate_fm(cs, csh2, csc2, bsz)
            fc = peer_fm(hc.astype(BF16), wq_t, keys_bf, u_bf, vt_bf)
            cs = _layer_norm_fm(_gated_residual_fm(cs, cg2, fc, alpha), ln_g[i, 1], ln_b[i, 1])
        fx = peer_fm(hx.astype(BF16), wq_t, keys_bf, u_bf, vt_bf)
        xs = _layer_norm_fm(_gated_residual_fm(xs, g2, fx, alpha), ln_g[i, 1], ln_b[i, 1])

    return xs.T.reshape(bsz, s_len, d)
```

```python
import functools
import math

import jax
import jax.numpy as jnp
from jax import lax
from jax.experimental import pallas as pl
from jax.experimental.pallas import tpu as pltpu

F32 = jnp.float32
BF16 = jnp.bfloat16
U32 = jnp.uint32

GRID_W = 64
LN_EPS = 1e-5
HEAD_DIM = 64
V_DIM = 2 * HEAD_DIM
ATTN_SCALE = HEAD_DIM ** -0.5
Q_SCALE_LOG2 = ATTN_SCALE * math.log2(math.e)
ROPE_BASE = 10000.0
ROPE_AXIS_PAIRS = HEAD_DIM // 4
CHUNK = 128
SGU_GROUPS = 4
POOL_WINDOWS = (2, 4, 8, 16)
P_HEADS = 8
N_KEYS = 128
P_TOPK = 16

LANE = 128
SUBLANE = 8
BF16_ROWS = 16
VMEM_LIMIT = 56 * 1024 * 1024
TOKEN_TILE = 512


def _params(*sem):
    return pltpu.CompilerParams(dimension_semantics=sem, vmem_limit_bytes=VMEM_LIMIT)


def _gelu_tanh(x):
    c1 = math.sqrt(2.0 / math.pi)
    c2 = c1 * 0.044715
    inner = x * (c1 + c2 * (x * x))
    return x * (0.5 + 0.5 * jnp.tanh(inner))


def _mm_kernel(*refs, n_pairs, has_bias):
    o_ref = refs[-1]
    acc = None
    for i in range(n_pairs):
        part = jnp.dot(refs[2 * i][...], refs[2 * i + 1][...], preferred_element_type=F32)
        acc = part if acc is None else acc + part
    if has_bias:
        acc = acc + refs[2 * n_pairs][...]
    o_ref[...] = acc.astype(o_ref.dtype)


def matmul_fm(pairs, out_dtype, bias=None, tn=TOKEN_TILE, tm=None):
    m = pairs[0][0].shape[0]
    n = pairs[0][1].shape[1]
    tn = min(tn, n)
    tm = m if tm is None else tm
    args, specs = [], []
    for w, x in pairs:
        k = w.shape[1]
        args += [w, x]
        specs += [pl.BlockSpec((tm, k), lambda j, i: (i, 0)),
                  pl.BlockSpec((k, tn), lambda j, i: (0, j))]
    if bias is not None:
        args.append(bias)
        specs.append(pl.BlockSpec((tm, 1), lambda j, i: (i, 0)))
    return pl.pallas_call(
        functools.partial(_mm_kernel, n_pairs=len(pairs), has_bias=bias is not None),
        grid=(n // tn, m // tm),
        in_specs=specs,
        out_specs=pl.BlockSpec((tm, tn), lambda j, i: (i, j)),
        out_shape=jax.ShapeDtypeStruct((m, n), out_dtype),
        compiler_params=_params("parallel", "parallel"),
        name="matmul_fm",
    )(*args)


def _ln_mod_kernel(*refs, alpha, has_resid, emit_x, emit_h):
    it = iter(refs)
    x = next(it)[...]
    if has_resid:
        y_ref, gate_ref, lng_ref, lnb_ref = next(it), next(it), next(it), next(it)
    if emit_h:
        shift_ref, scale_ref = next(it), next(it)
    if has_resid:
        z = alpha * x + gate_ref[...] * y_ref[...]
        mu = jnp.mean(z, axis=0, keepdims=True)
        zc = z - mu
        var = jnp.mean(zc * zc, axis=0, keepdims=True)
        x = zc * lax.rsqrt(var + LN_EPS) * lng_ref[...] + lnb_ref[...]
        if emit_x:
            next(it)[...] = x
    if emit_h:
        next(it)[...] = (x * (1.0 + scale_ref[...]) + shift_ref[...]).astype(BF16)


def ln_mod_fm(xs, *, resid=None, mod=None, alpha=1.0, emit_x=True, tn=TOKEN_TILE):
    d, n = xs.shape
    tn = min(tn, n)
    n_seg = (resid[1] if resid is not None else mod[0]).shape[1]
    tps = (n // n_seg) // tn
    tok = pl.BlockSpec((d, tn), lambda j: (0, j))
    seg = pl.BlockSpec((None, d, 1), lambda j: (j // tps, 0, 0))
    col = pl.BlockSpec((d, 1), lambda j: (0, 0))
    args, specs, out_shape, out_specs = [xs], [tok], [], []
    if resid is not None:
        y, gate, ln_g, ln_b = resid
        args += [y, gate.T[:, :, None], ln_g[:, None], ln_b[:, None]]
        specs += [tok, seg, col, col]
        if emit_x:
            out_shape.append(jax.ShapeDtypeStruct((d, n), F32))
            out_specs.append(tok)
    if mod is not None:
        args += [mod[0].T[:, :, None], mod[1].T[:, :, None]]
        specs += [seg, seg]
        out_shape.append(jax.ShapeDtypeStruct((d, n), BF16))
        out_specs.append(tok)
    outs = pl.pallas_call(
        functools.partial(_ln_mod_kernel, alpha=alpha, has_resid=resid is not None,
                          emit_x=emit_x and resid is not None, emit_h=mod is not None),
        grid=(n // tn,),
        in_specs=specs,
        out_specs=out_specs,
        out_shape=out_shape,
        compiler_params=_params("parallel"),
        name="ln_mod",
    )(*args)
    outs = list(outs)
    x_new = outs.pop(0) if (resid is not None and emit_x) else None
    h = outs.pop(0) if mod is not None else None
    return x_new, h


ATTN_LANES = 256
V_ROWS = V_DIM + BF16_ROWS


def _attn_kernel(lam_ref, q_ref, k_ref, v_ref, g_ref, o_ref, qp_ref, m_ref, acc_ref,
                 *, tq, n_chunks, out_scale):
    q = q_ref[...]
    row = lax.broadcasted_iota(jnp.int32, q.shape, 0)
    zero = jnp.zeros_like(q)
    qp_ref[:, :tq] = jnp.where(row < HEAD_DIM, q, zero)
    qp_ref[:, tq:] = jnp.where(row >= HEAD_DIM, q, zero)
    m_ref[...] = jnp.full(m_ref.shape, -jnp.inf, F32)
    acc_ref[...] = jnp.zeros(acc_ref.shape, F32)

    n_groups = 2 * tq // ATTN_LANES
    items = [(c, g) for c in range(n_chunks) for g in range(n_groups)]

    def scores(item):
        c, g = item
        return jnp.dot(k_ref[c], qp_ref[:, g * ATTN_LANES:(g + 1) * ATTN_LANES],
                       preferred_element_type=F32)

    def lanes_of(item):
        return slice(item[1] * ATTN_LANES, (item[1] + 1) * ATTN_LANES)

    def running_max(item, s):
        m_prev = m_ref[:, lanes_of(item)]
        m_new = jnp.maximum(m_prev, jnp.max(s, axis=0, keepdims=True))
        m_ref[:, lanes_of(item)] = m_new
        return m_new, jnp.exp2(m_prev - m_new)

    assert n_groups >= 2
    n_items = len(items)
    s_buf = {0: scores(items[0])}
    if n_items > 1:
        s_buf[1] = scores(items[1])
    stats = {0: running_max(items[0], s_buf[0])}
    for idx, item in enumerate(items):
        if idx + 2 < n_items:
            s_buf[idx + 2] = scores(items[idx + 2])
        if idx + 1 < n_items:
            stats[idx + 1] = running_max(items[idx + 1], s_buf[idx + 1])
        m_new, alpha = stats.pop(idx)
        p = jnp.exp2(s_buf.pop(idx) - m_new).astype(BF16)
        lanes = lanes_of(item)
        acc_ref[:, lanes] = acc_ref[:, lanes] * alpha + jnp.dot(v_ref[item[0]], p, preferred_element_type=F32)

    lam = lam_ref[0, 0]
    acc = acc_ref[...]
    inv = 1.0 / acc[V_DIM:V_DIM + 1, :]
    o = acc[:V_DIM, :tq] * inv[:, :tq] - lam * (acc[:V_DIM, tq:] * inv[:, tq:])
    ms = jnp.mean(o * o, axis=0, keepdims=True)
    o = o * lax.rsqrt(ms + LN_EPS) * g_ref[...] * out_scale
    o_ref[...] = o.astype(o_ref.dtype)


def diff_attention(q_fm, k_chunks, v_chunks, lam, norm_g, lam_init, n_batch, tq):
    n_heads = q_fm.shape[0] // V_DIM
    sq = q_fm.shape[1] // n_batch
    nq = sq // tq
    _, _, nc, tk, _ = k_chunks.shape
    return pl.pallas_call(
        functools.partial(_attn_kernel, tq=tq, n_chunks=nc, out_scale=1.0 - lam_init),
        grid=(n_batch, n_heads, nq),
        in_specs=[
            pl.BlockSpec(memory_space=pltpu.SMEM),
            pl.BlockSpec((V_DIM, tq), lambda b, h, i: (h, b * nq + i)),
            pl.BlockSpec((None, None, nc, tk, V_DIM), lambda b, h, i: (b, h, 0, 0, 0)),
            pl.BlockSpec((None, None, nc, V_ROWS, tk), lambda b, h, i: (b, h, 0, 0, 0)),
            pl.BlockSpec((V_DIM, 1), lambda b, h, i: (0, 0)),
        ],
        out_specs=pl.BlockSpec((V_DIM, tq), lambda b, h, i: (h, b * nq + i)),
        out_shape=jax.ShapeDtypeStruct(q_fm.shape, BF16),
        scratch_shapes=[
            pltpu.VMEM((V_DIM, 2 * tq), BF16),
            pltpu.VMEM((1, 2 * tq), F32),
            pltpu.VMEM((V_ROWS, 2 * tq), F32),
        ],
        compiler_params=_params("parallel", "parallel", "parallel"),
        name="diff_attention",
    )(lam, q_fm, k_chunks, v_chunks, norm_g)


def _ab_in_kernel(h_ref, w_ref, cos_ref, sin_ref, lng_ref, lnb_ref, wst_ref, bs_ref,
                  q_ref, k_ref, v_ref, g_ref, *, qk_w, sgu_half):
    p = jnp.dot(w_ref[...], h_ref[...], preferred_element_type=F32)
    cs, sn = cos_ref[...], sin_ref[...]
    half = HEAD_DIM // 2
    for blk in range(qk_w // HEAD_DIM):
        r = blk * HEAD_DIM
        a, b = p[r:r + half], p[r + half:r + HEAD_DIM]
        q_ref[r:r + half, :] = ((a * cs - b * sn) * Q_SCALE_LOG2).astype(BF16)
        q_ref[r + half:r + HEAD_DIM, :] = ((a * sn + b * cs) * Q_SCALE_LOG2).astype(BF16)
        a, b = p[qk_w + r:qk_w + r + half], p[qk_w + r + half:qk_w + r + HEAD_DIM]
        k_ref[r:r + half, :] = (a * cs - b * sn).astype(BF16)
        k_ref[r + half:r + HEAD_DIM, :] = (a * sn + b * cs).astype(BF16)
    v_ref[...] = p[2 * qk_w:3 * qk_w].astype(BF16)

    gu = p[3 * qk_w:3 * qk_w + sgu_half]
    v = _gelu_tanh(p[3 * qk_w + sgu_half:])
    mu = jnp.mean(v, axis=0, keepdims=True)
    vc = v - mu
    var = jnp.mean(vc * vc, axis=0, keepdims=True)
    vb = (vc * lax.rsqrt(var + LN_EPS) * lng_ref[...] + lnb_ref[...]).astype(BF16)
    gc = sgu_half // SGU_GROUPS
    for g in range(SGU_GROUPS):
        for j in range(vb.shape[1] // CHUNK):
            rows, cols = slice(g * gc, (g + 1) * gc), slice(j * CHUNK, (j + 1) * CHUNK)
            s = jnp.dot(vb[rows, cols], wst_ref[g], preferred_element_type=F32) + bs_ref[g]
            g_ref[rows, cols] = (_gelu_tanh(gu[rows, cols]) * s).astype(BF16)


def ab_in_proj(h_bf, w_in_t, cos_tab, sin_tab, ln_g, ln_b, ws_t, bs, tn=TOKEN_TILE):
    d, n = h_bf.shape
    tn = min(tn, n)
    qk_w = w_in_t.shape[0] // 4
    sgu_half = (w_in_t.shape[0] - 3 * qk_w) // 2
    n_pos = cos_tab.shape[1] // tn
    tok = lambda rows: pl.BlockSpec((rows, tn), lambda j: (0, j))
    pos = pl.BlockSpec((HEAD_DIM // 2, tn), lambda j: (0, j % n_pos))
    return pl.pallas_call(
        functools.partial(_ab_in_kernel, qk_w=qk_w, sgu_half=sgu_half),
        grid=(n // tn,),
        in_specs=[
            tok(d),
            pl.BlockSpec(w_in_t.shape, lambda j: (0, 0)),
            pos, pos,
            pl.BlockSpec((sgu_half, 1), lambda j: (0, 0)),
            pl.BlockSpec((sgu_half, 1), lambda j: (0, 0)),
            pl.BlockSpec((SGU_GROUPS, CHUNK, CHUNK), lambda j: (0, 0, 0)),
            pl.BlockSpec((SGU_GROUPS, 1, CHUNK), lambda j: (0, 0, 0)),
        ],
        out_specs=[tok(qk_w), tok(qk_w), tok(qk_w), tok(sgu_half)],
        out_shape=[jax.ShapeDtypeStruct((qk_w, n), BF16)] * 3 + [jax.ShapeDtypeStruct((sgu_half, n), BF16)],
        compiler_params=_params("parallel"),
        name="ab_in_proj",
    )(h_bf, w_in_t, cos_tab, sin_tab, ln_g, ln_b, ws_t, bs)


POOL_TILE = 256
POOL_HALO = 128


def _pool_kernel(left_ref, mid_ref, right_ref, wg_ref, scale_ref, wout_ref, o_ref, *, seq_len):
    j = pl.program_id(0)
    t0 = j * POOL_TILE
    seq_start = (t0 // seq_len) * seq_len
    win = POOL_TILE + 2 * POOL_HALO
    h_all = jnp.concatenate([left_ref[...], mid_ref[...], right_ref[...]], axis=1)
    h_hi = h_all.astype(BF16)
    h_lo = (h_all - h_hi.astype(F32)).astype(BF16)
    tau = t0 - POOL_HALO + lax.broadcasted_iota(jnp.int32, (win, POOL_TILE), 0)
    t = t0 + lax.broadcasted_iota(jnp.int32, (win, POOL_TILE), 1)
    gc = mid_ref.shape[0] // len(POOL_WINDOWS)
    outs = []
    for g, w in enumerate(POOL_WINDOWS):
        lo = jnp.maximum(t - w // 2, seq_start)
        hi = jnp.minimum(t + (w - w // 2), seq_start + seq_len)
        band = jnp.where((tau >= lo) & (tau < hi), 1.0, 0.0).astype(BF16)
        cnt = (hi - lo)[0:1, :].astype(F32)
        sl = slice(g * gc, (g + 1) * gc)
        ssum = (jnp.dot(h_hi[sl], band, preferred_element_type=F32)
                + jnp.dot(h_lo[sl], band, preferred_element_type=F32))
        m = ssum / cnt - mid_ref[sl, :]
        m = jnp.dot(wg_ref[g], m.astype(BF16), preferred_element_type=F32)
        outs.append(m)
    m_all = (jnp.concatenate(outs, axis=0) * scale_ref[...]).astype(BF16)
    o_ref[...] = jnp.dot(wout_ref[...], m_all, preferred_element_type=F32)


def pool_mixer_fm(h, wg_t, scale, wout_t, seq_len):
    c, n = h.shape
    nt = n // POOL_TILE
    r = POOL_TILE // POOL_HALO
    nh = n // POOL_HALO
    return pl.pallas_call(
        functools.partial(_pool_kernel, seq_len=seq_len),
        grid=(nt,),
        in_specs=[
            pl.BlockSpec((c, POOL_HALO), lambda j: (0, jnp.maximum(j * r - 1, 0))),
            pl.BlockSpec((c, POOL_TILE), lambda j: (0, j)),
            pl.BlockSpec((c, POOL_HALO), lambda j: (0, jnp.minimum(j * r + r, nh - 1))),
            pl.BlockSpec(wg_t.shape, lambda j: (0, 0, 0)),
            pl.BlockSpec((c, 1), lambda j: (0, 0)),
            pl.BlockSpec(wout_t.shape, lambda j: (0, 0)),
        ],
        out_specs=pl.BlockSpec((wout_t.shape[0], POOL_TILE), lambda j: (0, j)),
        out_shape=jax.ShapeDtypeStruct((wout_t.shape[0], n), F32),
        compiler_params=_params("parallel"),
        name="pool_mixer",
    )(h, h, h, wg_t, scale, wout_t)


def _sort16_desc(xs):
    xs = list(xs)
    n = len(xs)
    k = 2
    while k <= n:
        j = k // 2
        while j >= 1:
            for i in range(n):
                l = i ^ j
                if l > i:
                    hi, lo = jnp.maximum(xs[i], xs[l]), jnp.minimum(xs[i], xs[l])
                    if (i & k) == 0:
                        xs[i], xs[l] = hi, lo
                    else:
                        xs[i], xs[l] = lo, hi
            j //= 2
        k *= 2
    return xs


def _bitonic_merge_desc(xs):
    xs = list(xs)
    n = len(xs)
    j = n // 2
    while j >= 1:
        for i in range(n):
            l = i ^ j
            if l > i:
                xs[i], xs[l] = jnp.maximum(xs[i], xs[l]), jnp.minimum(xs[i], xs[l])
        j //= 2
    return xs


def _top16_of_two(a, b):
    n = len(a)
    return _bitonic_merge_desc([jnp.maximum(a[i], b[n - 1 - i]) for i in range(n)])


def _top16_desc(vals):
    groups = [_sort16_desc(vals[i:i + P_TOPK]) for i in range(0, len(vals), P_TOPK)]
    while len(groups) > 1:
        groups = [_top16_of_two(groups[i], groups[i + 1]) for i in range(0, len(groups), 2)]
    return groups[0]


def _bf16_pair_words(x):
    u = lax.bitcast_convert_type(x.astype(BF16).astype(F32), U32)
    return u | lax.shift_right_logical(u, jnp.full(u.shape, 16, U32))


def _peer_route_kernel(h_ref, wq_ref, keys_ref, lrow_ref, e1_ref, rank2_ref, e2_ref, s_s):
    q = jnp.dot(wq_ref[...], h_ref[...], preferred_element_type=F32).astype(BF16)
    for hp in range(2 * P_HEADS):
        h, p = divmod(hp, 2)
        s = jnp.dot(keys_ref[hp], q[hp * N_KEYS:(hp + 1) * N_KEYS, :], preferred_element_type=F32)
        s_s[p, pl.ds(h, N_KEYS, stride=P_HEADS), :] = s

    s1 = [s_s[0, k * P_HEADS:(k + 1) * P_HEADS, :] for k in range(N_KEYS)]
    s2 = [s_s[1, k * P_HEADS:(k + 1) * P_HEADS, :] for k in range(N_KEYS)]
    a = _top16_desc(s1)
    b = _top16_desc(s2)
    cand = [[a[r1] + b[r2] for r2 in range(P_TOPK)] for r1 in range(P_TOPK)]
    rows = [list(r) for r in cand]
    while len(rows) > 1:
        rows = [_top16_of_two(rows[i], rows[i + 1]) for i in range(0, len(rows), 2)]
    best = rows[0]
    thr = best[P_TOPK - 1]
    z = None
    for r in range(P_TOPK):
        e = jnp.exp(best[r] - best[0])
        z = e if z is None else z + e
    inv_z = 1.0 / z
    counts = []
    for r1 in range(P_TOPK):
        cnt = jnp.zeros_like(thr)
        for r2 in range(P_TOPK):
            cnt = cnt + jnp.where(cand[r1][r2] >= thr, 1.0, 0.0)
        counts.append(cnt)

    def per_key(k, carry):
        base = pl.multiple_of(k * P_HEADS, P_HEADS)
        s1k = s_s[0, pl.ds(base, P_HEADS), :]
        s2k = s_s[1, pl.ds(base, P_HEADS), :]
        lrow = jnp.zeros_like(thr)
        rank = jnp.zeros_like(thr)
        for r in range(P_TOPK):
            lrow = jnp.where(s1k == a[r], counts[r], lrow)
            rank = jnp.where(b[r] > s2k, float(r + 1), rank)
        dst = pl.ds(k, P_HEADS, stride=N_KEYS)
        lrow_ref[dst, :] = _bf16_pair_words(lrow)
        e1_ref[dst, :] = _bf16_pair_words(jnp.exp(s1k - a[0]))
        rank2_ref[dst, :] = rank
        e2_ref[dst, :] = jnp.exp(s2k - b[0]) * inv_z
        return carry

    lax.fori_loop(0, N_KEYS, per_key, 0, unroll=4)


def peer_route(h_bf, wq_t, keys_bf, tn=LANE):
    d, n = h_bf.shape
    rows = N_KEYS * P_HEADS
    spec = pl.BlockSpec((rows, tn), lambda j: (0, j))
    return pl.pallas_call(
        _peer_route_kernel,
        grid=(n // tn,),
        in_specs=[
            pl.BlockSpec((d, tn), lambda j: (0, j)),
            pl.BlockSpec(wq_t.shape, lambda j: (0, 0)),
            pl.BlockSpec(keys_bf.shape, lambda j: (0, 0, 0)),
        ],
        out_specs=[spec] * 4,
        out_shape=[jax.ShapeDtypeStruct((rows, n), U32)] * 2 + [jax.ShapeDtypeStruct((rows, n), F32)] * 2,
        scratch_shapes=[pltpu.VMEM((2, rows, tn), F32)],
        compiler_params=_params("parallel"),
        name="peer_route",
    )(h_bf, wq_t, keys_bf)


PEER_LANES = 256


def _peer_dense_kernel(h_ref, u_ref, vt_ref, lrow_ref, e1_ref, rank2_ref, e2_ref, o_ref,
                       r2_s, e2_s, w_s, *, i1_tile):
    e = pl.program_id(1)
    tn = h_ref.shape[1]
    slabs = N_KEYS // BF16_ROWS

    @pl.when(e == 0)
    def _():
        o_ref[...] = jnp.zeros(o_ref.shape, F32)
        for h in range(P_HEADS):
            rows = slice(h * N_KEYS, (h + 1) * N_KEYS)
            r2_s[h] = rank2_ref[rows, :].astype(BF16).reshape(slabs, BF16_ROWS, tn)
            e2_s[h] = e2_ref[rows, :].astype(BF16).reshape(slabs, BF16_ROWS, tn)

    zero = jnp.zeros((), BF16)
    n_groups = tn // PEER_LANES

    def pre_act(g):
        return jnp.dot(u_ref[...], h_ref[:, g * PEER_LANES:(g + 1) * PEER_LANES],
                       preferred_element_type=F32)

    x_next = pre_act(0)
    for g in range(n_groups):
        x = x_next
        if g + 1 < n_groups:
            x_next = pre_act(g + 1)
        for il in range(i1_tile):
            for lt in range(PEER_LANES // LANE):
                lanes = slice(g * PEER_LANES + lt * LANE, g * PEER_LANES + (lt + 1) * LANE)
                gate = None
                for h in range(P_HEADS):
                    lb = pltpu.bitcast(jnp.broadcast_to(lrow_ref[h, il:il + 1, lanes], (SUBLANE, LANE)), BF16)
                    eb = pltpu.bitcast(jnp.broadcast_to(e1_ref[h, il:il + 1, lanes], (SUBLANE, LANE)), BF16)
                    term = jnp.where(r2_s[h, :, :, lanes] < lb, e2_s[h, :, :, lanes], zero) * eb
                    gate = term if gate is None else gate + term
                act = _gelu_tanh(x[il * N_KEYS:(il + 1) * N_KEYS, lt * LANE:(lt + 1) * LANE].astype(BF16))
                w = gate * act.reshape(slabs, BF16_ROWS, LANE)
                w_s[il * N_KEYS:(il + 1) * N_KEYS, lanes] = w.reshape(N_KEYS, LANE)
        group = slice(g * PEER_LANES, (g + 1) * PEER_LANES)
        o_ref[:, group] += jnp.dot(vt_ref[...], w_s[:, group], preferred_element_type=F32)


def peer_dense(h_bf, u_bf, vt_bf, lrow, e1, rank2, e2, tn=TOKEN_TILE, i1_tile=16):
    d, n = h_bf.shape
    n_exp = u_bf.shape[0]
    tn = min(tn, n)
    et = i1_tile * N_KEYS
    rows = N_KEYS * P_HEADS
    slabs = N_KEYS // BF16_ROWS
    words = lambda t: t.reshape(P_HEADS, N_KEYS, n)
    return pl.pallas_call(
        functools.partial(_peer_dense_kernel, i1_tile=i1_tile),
        grid=(n // tn, n_exp // et),
        in_specs=[
            pl.BlockSpec((d, tn), lambda j, e: (0, j)),
            pl.BlockSpec((et, d), lambda j, e: (e, 0)),
            pl.BlockSpec((d, et), lambda j, e: (0, e)),
            pl.BlockSpec((P_HEADS, i1_tile, tn), lambda j, e: (0, e, j)),
            pl.BlockSpec((P_HEADS, i1_tile, tn), lambda j, e: (0, e, j)),
            pl.BlockSpec((rows, tn), lambda j, e: (0, j)),
            pl.BlockSpec((rows, tn), lambda j, e: (0, j)),
        ],
        out_specs=pl.BlockSpec((d, tn), lambda j, e: (0, j)),
        out_shape=jax.ShapeDtypeStruct((d, n), F32),
        scratch_shapes=[
            pltpu.VMEM((P_HEADS, slabs, BF16_ROWS, tn), BF16),
            pltpu.VMEM((P_HEADS, slabs, BF16_ROWS, tn), BF16),
            pltpu.VMEM((et, tn), BF16),
        ],
        compiler_params=_params("parallel", "arbitrary"),
        name="peer_dense",
    )(h_bf, u_bf, vt_bf, words(lrow), words(e1), rank2, e2)


def peer_fm(h_bf, wq_t, keys_bf, u_bf, vt_bf):
    lrow, e1, rank2, e2 = peer_route(h_bf, wq_t, keys_bf)
    return peer_dense(h_bf, u_bf, vt_bf, lrow, e1, rank2, e2)


def _rope_tables_fm(rows):
    row = jnp.repeat(jnp.arange(rows, dtype=F32), GRID_W)
    col = jnp.tile(jnp.arange(GRID_W, dtype=F32), rows)
    inv = ROPE_BASE ** (-jnp.arange(ROPE_AXIS_PAIRS, dtype=F32) / ROPE_AXIS_PAIRS)
    ang = jnp.concatenate([inv[:, None] * row[None, :], inv[:, None] * col[None, :]], axis=0)
    return jnp.cos(ang), jnp.sin(ang)


def _kv_chunks(kc, kx, vc, vx, n_batch, tk):
    rows = kx.shape[0]
    nh = rows // V_DIM

    def per_batch(c_part, x_part):
        parts = [x_part.reshape(nh, V_DIM, n_batch, -1)]
        if c_part is not None:
            parts = [c_part.reshape(nh, V_DIM, n_batch, -1)] + parts
        return jnp.concatenate(parts, axis=-1)

    k = per_batch(kc, kx)
    v = per_batch(vc, vx)
    l = k.shape[-1]
    k = jnp.transpose(k, (2, 0, 3, 1)).reshape(n_batch, nh, l // tk, tk, V_DIM)
    extra = jnp.zeros((nh, BF16_ROWS, n_batch, l), v.dtype).at[:, 0].set(1.0)
    v = jnp.concatenate([v, extra], axis=1)
    v = jnp.transpose(v.reshape(nh, V_ROWS, n_batch, l // tk, tk), (2, 0, 3, 1, 4))
    return k, v


def _pick_tk(l):
    for tk in (768, 512, 256, 128):
        if l % tk == 0:
            return tk
    raise ValueError(f"unsupported key length {l}")


def kernel(x, c, ctx, c_ctx, ada_w, ada_b, ln_g, ln_b, ab_w_in, ab_w_out, diff_lam, diff_norm_g,
           sgu_ln_g, sgu_ln_b, sgu_w, sgu_b, pool_w_in, pool_w_grp, pool_scale, pool_w_out,
           peer_wq, peer_keys, peer_u, peer_v):
    bsz, s_len, d = x.shape
    c_len = ctx.shape[1]
    depth = ada_w.shape[0]
    qk_w = ab_w_in.shape[2] // 4
    alpha = (2.0 * depth) ** 0.25
    last_ctx_read = 2 * ((depth - 1) // 2)
    cos, sin = _rope_tables_fm(s_len // GRID_W)
    c_tile = min(TOKEN_TILE, bsz * c_len)
    cos_c = jnp.ones((HEAD_DIM // 2, c_tile), F32)
    sin_c = jnp.zeros((HEAD_DIM // 2, c_tile), F32)

    xs = x.reshape(bsz * s_len, d).T
    cs = ctx.reshape(bsz * c_len, d).T

    cond = jnp.zeros((d, LANE), F32)
    cond = cond.at[:, :bsz].set(jax.nn.silu(c).T).at[:, bsz].set(jax.nn.silu(c_ctx))
    cond = cond.astype(BF16)
    mods_x, mods_c = [], []
    for i in range(depth):
        ada = matmul_fm([(ada_w[i].T.astype(BF16), cond)], F32, bias=ada_b[i][:, None])
        mods_x.append([ada[k * d:(k + 1) * d, :bsz] for k in range(6)])
        mods_c.append([ada[k * d:(k + 1) * d, bsz:bsz + 1] for k in range(6)])

    def uses_ctx(i):
        return i < depth and (i < last_ctx_read or i % 2 == 0)

    _, hx = ln_mod_fm(xs, mod=(mods_x[0][0], mods_x[0][1]))
    hc = ln_mod_fm(cs, mod=(mods_c[0][0], mods_c[0][1]))[1] if uses_ctx(0) else None

    for i in range(depth):
        j = i // 2
        even = i % 2 == 0
        ctx_out = i < last_ctx_read
        last = i == depth - 1
        sh1, sc1, g1, sh2, sc2, g2 = mods_x[i]
        csh1, csc1, cg1, csh2, csc2, cg2 = mods_c[i]

        if even:
            lam_init = 0.8 - 0.6 * math.exp(-0.3 * i)
            lq1, lk1, lq2, lk2 = diff_lam[j].astype(F32)
            lam = (jnp.exp(jnp.sum(lq1 * lk1)) - jnp.exp(jnp.sum(lq2 * lk2)) + lam_init).reshape(1, 1)
            w_in_t = ab_w_in[j].T.astype(BF16)
            w_out_t = ab_w_out[j].T.astype(BF16)
            norm_g = diff_norm_g[j][:, None]
            sgu_args = (sgu_ln_g[j][:, None], sgu_ln_b[j][:, None],
                        jnp.transpose(sgu_w[j], (0, 2, 1)).astype(BF16), sgu_b[j][:, None, :])
            q_x, k_x, v_x, g_x = ab_in_proj(hx, w_in_t, cos, sin, *sgu_args)
            q_c, k_c, v_c, g_c = ab_in_proj(hc, w_in_t, cos_c, sin_c, *sgu_args)
            k_all, v_all = _kv_chunks(k_c, k_x, v_c, v_x, bsz, _pick_tk(c_len + s_len))
            a_x = diff_attention(q_x, k_all, v_all, lam, norm_g, lam_init, bsz, tq=min(512, s_len))
            yx = matmul_fm([(w_out_t[:, :qk_w], a_x), (w_out_t[:, qk_w:], g_x)], F32)
            if ctx_out:
                kc_only, vc_only = _kv_chunks(None, k_c, None, v_c, bsz, _pick_tk(c_len))
                a_c = diff_attention(q_c, kc_only, vc_only, lam, norm_g, lam_init, bsz, tq=c_len)
                yc = matmul_fm([(w_out_t[:, :qk_w], a_c), (w_out_t[:, qk_w:], g_c)], F32)
        else:
            p_in_t = pool_w_in[j].T.astype(BF16)
            wg_t = jnp.transpose(pool_w_grp[j], (0, 2, 1)).astype(BF16)
            p_out_t = pool_w_out[j].T.astype(BF16)
            scale = pool_scale[j][:, None]
            yx = pool_mixer_fm(matmul_fm([(p_in_t, hx)], F32), wg_t, scale, p_out_t, s_len)
            if ctx_out:
                yc = pool_mixer_fm(matmul_fm([(p_in_t, hc)], F32), wg_t, scale, p_out_t, c_len)

        wq_t = peer_wq[i].T.astype(BF16)
        keys_bf = peer_keys[i].reshape(2 * P_HEADS, N_KEYS, -1).astype(BF16)
        u_bf = peer_u[i].astype(BF16)
        vt_bf = peer_v[i].T.astype(BF16)
        next_mod_c = (mods_c[i + 1][0], mods_c[i + 1][1]) if uses_ctx(i + 1) else None
        if ctx_out:
            cs, hc = ln_mod_fm(cs, resid=(yc, cg1, ln_g[i, 0], ln_b[i, 0]), mod=(csh2, csc2), alpha=alpha)
            fc = peer_fm(hc, wq_t, keys_bf, u_bf, vt_bf)
            if uses_ctx(i + 1):
                cs, hc = ln_mod_fm(cs, resid=(fc, cg2, ln_g[i, 1], ln_b[i, 1]), mod=next_mod_c, alpha=alpha)
        elif next_mod_c is not None:
            hc = ln_mod_fm(cs, mod=next_mod_c)[1]

        xs, hx = ln_mod_fm(xs, resid=(yx, g1, ln_g[i, 0], ln_b[i, 0]), mod=(sh2, sc2), alpha=alpha)
        fx = peer_fm(hx, wq_t, keys_bf, u_bf, vt_bf)
        next_mod_x = None if last else (mods_x[i + 1][0], mods_x[i + 1][1])
        xs, hx = ln_mod_fm(xs, resid=(fx, g2, ln_g[i, 1], ln_b[i, 1]), mod=next_mod_x, alpha=alpha)

    return xs.T.reshape(bsz, s_len, d)
```

```python
import functools
import math

import jax
import jax.numpy as jnp
from jax import lax
from jax.experimental import pallas as pl
from jax.experimental.pallas import tpu as pltpu

F32 = jnp.float32
BF16 = jnp.bfloat16
U32 = jnp.uint32

GRID_W = 64
LN_EPS = 1e-5
HEAD_DIM = 64
V_DIM = 2 * HEAD_DIM
ATTN_SCALE = HEAD_DIM ** -0.5
Q_SCALE_LOG2 = ATTN_SCALE * math.log2(math.e)
ROPE_BASE = 10000.0
ROPE_AXIS_PAIRS = HEAD_DIM // 4
CHUNK = 128
SGU_GROUPS = 4
POOL_WINDOWS = (2, 4, 8, 16)
P_HEADS = 8
N_KEYS = 128
P_TOPK = 16

LANE = 128
SUBLANE = 8
BF16_ROWS = 16
VMEM_LIMIT = 56 * 1024 * 1024
TOKEN_TILE = 512


def _params(*sem):
    return pltpu.CompilerParams(dimension_semantics=sem, vmem_limit_bytes=VMEM_LIMIT)


def _gelu_tanh(x):
    c1 = math.sqrt(2.0 / math.pi)
    c2 = c1 * 0.044715
    inner = x * (c1 + c2 * (x * x))
    return x * (0.5 + 0.5 * jnp.tanh(inner))


def _mm_kernel(*refs, n_pairs, has_bias):
    o_ref = refs[-1]
    acc = None
    for i in range(n_pairs):
        part = jnp.dot(refs[2 * i][...], refs[2 * i + 1][...], preferred_element_type=F32)
        acc = part if acc is None else acc + part
    if has_bias:
        acc = acc + refs[2 * n_pairs][...]
    o_ref[...] = acc.astype(o_ref.dtype)


def matmul_fm(pairs, out_dtype, bias=None, tn=TOKEN_TILE, tm=None):
    m = pairs[0][0].shape[0]
    n = pairs[0][1].shape[1]
    tn = min(tn, n)
    tm = m if tm is None else tm
    args, specs = [], []
    for w, x in pairs:
        k = w.shape[1]
        args += [w, x]
        specs += [pl.BlockSpec((tm, k), lambda j, i: (i, 0)),
                  pl.BlockSpec((k, tn), lambda j, i: (0, j))]
    if bias is not None:
        args.append(bias)
        specs.append(pl.BlockSpec((tm, 1), lambda j, i: (i, 0)))
    return pl.pallas_call(
        functools.partial(_mm_kernel, n_pairs=len(pairs), has_bias=bias is not None),
        grid=(n // tn, m // tm),
        in_specs=specs,
        out_specs=pl.BlockSpec((tm, tn), lambda j, i: (i, j)),
        out_shape=jax.ShapeDtypeStruct((m, n), out_dtype),
        compiler_params=_params("parallel", "parallel"),
        name="matmul_fm",
    )(*args)


def _ln_mod_kernel(*refs, alpha, has_resid, emit_x, emit_h):
    it = iter(refs)
    x = next(it)[...]
    if has_resid:
        y_ref, gate_ref, lng_ref, lnb_ref = next(it), next(it), next(it), next(it)
    if emit_h:
        shift_ref, scale_ref = next(it), next(it)
    if has_resid:
        z = alpha * x + gate_ref[...] * y_ref[...]
        mu = jnp.mean(z, axis=0, keepdims=True)
        zc = z - mu
        var = jnp.mean(zc * zc, axis=0, keepdims=True)
        x = zc * lax.rsqrt(var + LN_EPS) * lng_ref[...] + lnb_ref[...]
        if emit_x:
            next(it)[...] = x
    if emit_h:
        next(it)[...] = (x * (1.0 + scale_ref[...]) + shift_ref[...]).astype(BF16)


def ln_mod_fm(xs, *, resid=None, mod=None, alpha=1.0, emit_x=True, tn=TOKEN_TILE):
    d, n = xs.shape
    tn = min(tn, n)
    n_seg = (resid[1] if resid is not None else mod[0]).shape[1]
    tps = (n // n_seg) // tn
    tok = pl.BlockSpec((d, tn), lambda j: (0, j))
    seg = pl.BlockSpec((None, d, 1), lambda j: (j // tps, 0, 0))
    col = pl.BlockSpec((d, 1), lambda j: (0, 0))
    args, specs, out_shape, out_specs = [xs], [tok], [], []
    if resid is not None:
        y, gate, ln_g, ln_b = resid
        args += [y, gate.T[:, :, None], ln_g[:, None], ln_b[:, None]]
        specs += [tok, seg, col, col]
        if emit_x:
            out_shape.append(jax.ShapeDtypeStruct((d, n), F32))
            out_specs.append(tok)
    if mod is not None:
        args += [mod[0].T[:, :, None], mod[1].T[:, :, None]]
        specs += [seg, seg]
        out_shape.append(jax.ShapeDtypeStruct((d, n), BF16))
        out_specs.append(tok)
    outs = pl.pallas_call(
        functools.partial(_ln_mod_kernel, alpha=alpha, has_resid=resid is not None,
                          emit_x=emit_x and resid is not None, emit_h=mod is not None),
        grid=(n // tn,),
        in_specs=specs,
        out_specs=out_specs,
        out_shape=out_shape,
        compiler_params=_params("parallel"),
        name="ln_mod",
    )(*args)
    outs = list(outs)
    x_new = outs.pop(0) if (resid is not None and emit_x) else None
    h = outs.pop(0) if mod is not None else None
    return x_new, h


ATTN_LANES = 256
V_ROWS = V_DIM + BF16_ROWS


def _attn_kernel(lam_ref, q_ref, k_ref, v_ref, g_ref, o_ref, qp_ref, m_ref, acc_ref,
                 *, tq, n_chunks, out_scale):
    q = q_ref[...]
    row = lax.broadcasted_iota(jnp.int32, q.shape, 0)
    zero = jnp.zeros_like(q)
    qp_ref[:, :tq] = jnp.where(row < HEAD_DIM, q, zero)
    qp_ref[:, tq:] = jnp.where(row >= HEAD_DIM, q, zero)
    m_ref[...] = jnp.full(m_ref.shape, -jnp.inf, F32)
    acc_ref[...] = jnp.zeros(acc_ref.shape, F32)

    n_groups = 2 * tq // ATTN_LANES
    items = [(c, g) for c in range(n_chunks) for g in range(n_groups)]

    def scores(item):
        c, g = item
        return jnp.dot(k_ref[c], qp_ref[:, g * ATTN_LANES:(g + 1) * ATTN_LANES],
                       preferred_element_type=F32)

    def lanes_of(item):
        return slice(item[1] * ATTN_LANES, (item[1] + 1) * ATTN_LANES)

    def running_max(item, s):
        m_prev = m_ref[:, lanes_of(item)]
        m_new = jnp.maximum(m_prev, jnp.max(s, axis=0, keepdims=True))
        m_ref[:, lanes_of(item)] = m_new
        return m_new, jnp.exp2(m_prev - m_new)

    assert n_groups >= 2
    n_items = len(items)
    s_buf = {0: scores(items[0])}
    if n_items > 1:
        s_buf[1] = scores(items[1])
    stats = {0: running_max(items[0], s_buf[0])}
    for idx, item in enumerate(items):
        if idx + 2 < n_items:
            s_buf[idx + 2] = scores(items[idx + 2])
        if idx + 1 < n_items:
            stats[idx + 1] = running_max(items[idx + 1], s_buf[idx + 1])
        m_new, alpha = stats.pop(idx)
        p = jnp.exp2(s_buf.pop(idx) - m_new).astype(BF16)
        lanes = lanes_of(item)
        acc_ref[:, lanes] = acc_ref[:, lanes] * alpha + jnp.dot(v_ref[item[0]], p, preferred_element_type=F32)

    lam = lam_ref[0, 0]
    acc = acc_ref[...]
    inv = 1.0 / acc[V_DIM:V_DIM + 1, :]
    o = acc[:V_DIM, :tq] * inv[:, :tq] - lam * (acc[:V_DIM, tq:] * inv[:, tq:])
    ms = jnp.mean(o * o, axis=0, keepdims=True)
    o = o * lax.rsqrt(ms + LN_EPS) * g_ref[...] * out_scale
    o_ref[...] = o.astype(o_ref.dtype)


def diff_attention(q_fm, k_chunks, v_chunks, lam, norm_g, lam_init, n_batch, tq):
    n_heads = q_fm.shape[0] // V_DIM
    sq = q_fm.shape[1] // n_batch
    nq = sq // tq
    _, _, nc, tk, _ = k_chunks.shape
    return pl.pallas_call(
        functools.partial(_attn_kernel, tq=tq, n_chunks=nc, out_scale=1.0 - lam_init),
        grid=(n_batch, n_heads, nq),
        in_specs=[
            pl.BlockSpec(memory_space=pltpu.SMEM),
            pl.BlockSpec((V_DIM, tq), lambda b, h, i: (h, b * nq + i)),
            pl.BlockSpec((None, None, nc, tk, V_DIM), lambda b, h, i: (b, h, 0, 0, 0)),
            pl.BlockSpec((None, None, nc, V_ROWS, tk), lambda b, h, i: (b, h, 0, 0, 0)),
            pl.BlockSpec((V_DIM, 1), lambda b, h, i: (0, 0)),
        ],
        out_specs=pl.BlockSpec((V_DIM, tq), lambda b, h, i: (h, b * nq + i)),
        out_shape=jax.ShapeDtypeStruct(q_fm.shape, BF16),
        scratch_shapes=[
            pltpu.VMEM((V_DIM, 2 * tq), BF16),
            pltpu.VMEM((1, 2 * tq), F32),
            pltpu.VMEM((V_ROWS, 2 * tq), F32),
        ],
        compiler_params=_params("parallel", "parallel", "parallel"),
        name="diff_attention",
    )(lam, q_fm, k_chunks, v_chunks, norm_g)


def _ab_in_kernel(h_ref, w_ref, cos_ref, sin_ref, lng_ref, lnb_ref, wst_ref, bs_ref,
                  q_ref, k_ref, v_ref, g_ref, *, qk_w, sgu_half):
    p = jnp.dot(w_ref[...], h_ref[...], preferred_element_type=F32)
    cs, sn = cos_ref[...], sin_ref[...]
    half = HEAD_DIM // 2
    for blk in range(qk_w // HEAD_DIM):
        r = blk * HEAD_DIM
        a, b = p[r:r + half], p[r + half:r + HEAD_DIM]
        q_ref[r:r + half, :] = ((a * cs - b * sn) * Q_SCALE_LOG2).astype(BF16)
        q_ref[r + half:r + HEAD_DIM, :] = ((a * sn + b * cs) * Q_SCALE_LOG2).astype(BF16)
        a, b = p[qk_w + r:qk_w + r + half], p[qk_w + r + half:qk_w + r + HEAD_DIM]
        k_ref[r:r + half, :] = (a * cs - b * sn).astype(BF16)
        k_ref[r + half:r + HEAD_DIM, :] = (a * sn + b * cs).astype(BF16)
    v_ref[...] = p[2 * qk_w:3 * qk_w].astype(BF16)

    gu = p[3 * qk_w:3 * qk_w + sgu_half]
    v = _gelu_tanh(p[3 * qk_w + sgu_half:])
    mu = jnp.mean(v, axis=0, keepdims=True)
    vc = v - mu
    var = jnp.mean(vc * vc, axis=0, keepdims=True)
    vb = (vc * lax.rsqrt(var + LN_EPS) * lng_ref[...] + lnb_ref[...]).astype(BF16)
    gc = sgu_half // SGU_GROUPS
    for g in range(SGU_GROUPS):
        for j in range(vb.shape[1] // CHUNK):
            rows, cols = slice(g * gc, (g + 1) * gc), slice(j * CHUNK, (j + 1) * CHUNK)
            s = jnp.dot(vb[rows, cols], wst_ref[g], preferred_element_type=F32) + bs_ref[g]
            g_ref[rows, cols] = (_gelu_tanh(gu[rows, cols]) * s).astype(BF16)


def ab_in_proj(h_bf, w_in_t, cos_tab, sin_tab, ln_g, ln_b, ws_t, bs, tn=TOKEN_TILE):
    d, n = h_bf.shape
    tn = min(tn, n)
    qk_w = w_in_t.shape[0] // 4
    sgu_half = (w_in_t.shape[0] - 3 * qk_w) // 2
    n_pos = cos_tab.shape[1] // tn
    tok = lambda rows: pl.BlockSpec((rows, tn), lambda j: (0, j))
    pos = pl.BlockSpec((HEAD_DIM // 2, tn), lambda j: (0, j % n_pos))
    return pl.pallas_call(
        functools.partial(_ab_in_kernel, qk_w=qk_w, sgu_half=sgu_half),
        grid=(n // tn,),
        in_specs=[
            tok(d),
            pl.BlockSpec(w_in_t.shape, lambda j: (0, 0)),
            pos, pos,
            pl.BlockSpec((sgu_half, 1), lambda j: (0, 0)),
            pl.BlockSpec((sgu_half, 1), lambda j: (0, 0)),
            pl.BlockSpec((SGU_GROUPS, CHUNK, CHUNK), lambda j: (0, 0, 0)),
            pl.BlockSpec((SGU_GROUPS, 1, CHUNK), lambda j: (0, 0, 0)),
        ],
        out_specs=[tok(qk_w), tok(qk_w), tok(qk_w), tok(sgu_half)],
        out_shape=[jax.ShapeDtypeStruct((qk_w, n), BF16)] * 3 + [jax.ShapeDtypeStruct((sgu_half, n), BF16)],
        compiler_params=_params("parallel"),
        name="ab_in_proj",
    )(h_bf, w_in_t, cos_tab, sin_tab, ln_g, ln_b, ws_t, bs)


POOL_TILE = 256
POOL_HALO = 128


def _pool_kernel(left_ref, mid_ref, right_ref, wg_ref, scale_ref, wout_ref, o_ref, *, seq_len):
    j = pl.program_id(0)
    t0 = j * POOL_TILE
    seq_start = (t0 // seq_len) * seq_len
    win = POOL_TILE + 2 * POOL_HALO
    h_all = jnp.concatenate([left_ref[...], mid_ref[...], right_ref[...]], axis=1)
    h_hi = h_all.astype(BF16)
    h_lo = (h_all - h_hi.astype(F32)).astype(BF16)
    tau = t0 - POOL_HALO + lax.broadcasted_iota(jnp.int32, (win, POOL_TILE), 0)
    t = t0 + lax.broadcasted_iota(jnp.int32, (win, POOL_TILE), 1)
    gc = mid_ref.shape[0] // len(POOL_WINDOWS)
    outs = []
    for g, w in enumerate(POOL_WINDOWS):
        lo = jnp.maximum(t - w // 2, seq_start)
        hi = jnp.minimum(t + (w - w // 2), seq_start + seq_len)
        band = jnp.where((tau >= lo) & (tau < hi), 1.0, 0.0).astype(BF16)
        cnt = (hi - lo)[0:1, :].astype(F32)
        sl = slice(g * gc, (g + 1) * gc)
        ssum = (jnp.dot(h_hi[sl], band, preferred_element_type=F32)
                + jnp.dot(h_lo[sl], band, preferred_element_type=F32))
        m = ssum / cnt - mid_ref[sl, :]
        m = jnp.dot(wg_ref[g], m.astype(BF16), preferred_element_type=F32)
        outs.append(m)
    m_all = (jnp.concatenate(outs, axis=0) * scale_ref[...]).astype(BF16)
    o_ref[...] = jnp.dot(wout_ref[...], m_all, preferred_element_type=F32)


def pool_mixer_fm(h, wg_t, scale, wout_t, seq_len):
    c, n = h.shape
    nt = n // POOL_TILE
    r = POOL_TILE // POOL_HALO
    nh = n // POOL_HALO
    return pl.pallas_call(
        functools.partial(_pool_kernel, seq_len=seq_len),
        grid=(nt,),
        in_specs=[
            pl.BlockSpec((c, POOL_HALO), lambda j: (0, jnp.maximum(j * r - 1, 0))),
            pl.BlockSpec((c, POOL_TILE), lambda j: (0, j)),
            pl.BlockSpec((c, POOL_HALO), lambda j: (0, jnp.minimum(j * r + r, nh - 1))),
            pl.BlockSpec(wg_t.shape, lambda j: (0, 0, 0)),
            pl.BlockSpec((c, 1), lambda j: (0, 0)),
            pl.BlockSpec(wout_t.shape, lambda j: (0, 0)),
        ],
        out_specs=pl.BlockSpec((wout_t.shape[0], POOL_TILE), lambda j: (0, j)),
        out_shape=jax.ShapeDtypeStruct((wout_t.shape[0], n), F32),
        compiler_params=_params("parallel"),
        name="pool_mixer",
    )(h, h, h, wg_t, scale, wout_t)


def _sort16_desc(xs):
    xs = list(xs)
    n = len(xs)
    k = 2
    while k <= n:
        j = k // 2
        while j >= 1:
            for i in range(n):
                l = i ^ j
                if l > i:
                    hi, lo = jnp.maximum(xs[i], xs[l]), jnp.minimum(xs[i], xs[l])
                    if (i & k) == 0:
                        xs[i], xs[l] = hi, lo
                    else:
                        xs[i], xs[l] = lo, hi
            j //= 2
        k *= 2
    return xs


def _bitonic_merge_desc(xs):
    xs = list(xs)
    n = len(xs)
    j = n // 2
    while j >= 1:
        for i in range(n):
            l = i ^ j
            if l > i:
                xs[i], xs[l] = jnp.maximum(xs[i], xs[l]), jnp.minimum(xs[i], xs[l])
        j //= 2
    return xs


def _top16_of_two(a, b):
    n = len(a)
    return _bitonic_merge_desc([jnp.maximum(a[i], b[n - 1 - i]) for i in range(n)])


def _top16_desc(vals):
    groups = [_sort16_desc(vals[i:i + P_TOPK]) for i in range(0, len(vals), P_TOPK)]
    while len(groups) > 1:
        groups = [_top16_of_two(groups[i], groups[i + 1]) for i in range(0, len(groups), 2)]
    return groups[0]


def _bf16_pair_words(x):
    u = lax.bitcast_convert_type(x.astype(BF16).astype(F32), U32)
    return u | lax.shift_right_logical(u, jnp.full(u.shape, 16, U32))


def _peer_route_kernel(h_ref, wq_ref, keys_ref, lrow_ref, e1_ref, rank2_ref, e2_ref, s_s):
    q = jnp.dot(wq_ref[...], h_ref[...], preferred_element_type=F32).astype(BF16)
    for hp in range(2 * P_HEADS):
        h, p = divmod(hp, 2)
        s = jnp.dot(keys_ref[hp], q[hp * N_KEYS:(hp + 1) * N_KEYS, :], preferred_element_type=F32)
        s_s[p, pl.ds(h, N_KEYS, stride=P_HEADS), :] = s

    s1 = [s_s[0, k * P_HEADS:(k + 1) * P_HEADS, :] for k in range(N_KEYS)]
    s2 = [s_s[1, k * P_HEADS:(k + 1) * P_HEADS, :] for k in range(N_KEYS)]
    a = _top16_desc(s1)
    b = _top16_desc(s2)
    cand = [[a[r1] + b[r2] for r2 in range(P_TOPK)] for r1 in range(P_TOPK)]
    rows = [list(r) for r in cand]
    while len(rows) > 1:
        rows = [_top16_of_two(rows[i], rows[i + 1]) for i in range(0, len(rows), 2)]
    best = rows[0]
    thr = best[P_TOPK - 1]
    z = None
    for r in range(P_TOPK):
        e = jnp.exp(best[r] - best[0])
        z = e if z is None else z + e
    inv_z = 1.0 / z
    counts = []
    for r1 in range(P_TOPK):
        cnt = jnp.zeros_like(thr)
        for r2 in range(P_TOPK):
            cnt = cnt + jnp.where(cand[r1][r2] >= thr, 1.0, 0.0)
        counts.append(cnt)

    def per_key(k, carry):
        base = pl.multiple_of(k * P_HEADS, P_HEADS)
        s1k = s_s[0, pl.ds(base, P_HEADS), :]
        s2k = s_s[1, pl.ds(base, P_HEADS), :]
        lrow = jnp.zeros_like(thr)
        rank = jnp.zeros_like(thr)
        for r in range(P_TOPK):
            lrow = jnp.where(s1k == a[r], counts[r], lrow)
            rank = jnp.where(b[r] > s2k, float(r + 1), rank)
        lrow_ref[k] = _bf16_pair_words(lrow)
        e1_ref[k] = _bf16_pair_words(jnp.exp(s1k - a[0]))
        rank2_ref[k] = rank
        e2_ref[k] = jnp.exp(s2k - b[0]) * inv_z
        return carry

    lax.fori_loop(0, N_KEYS, per_key, 0, unroll=4)


def peer_route(h_bf, wq_t, keys_bf, tn=LANE):
    d, n = h_bf.shape
    rows = N_KEYS * P_HEADS
    spec = pl.BlockSpec((N_KEYS, P_HEADS, tn), lambda j: (0, 0, j))
    sheet = lambda dt: jax.ShapeDtypeStruct((N_KEYS, P_HEADS, n), dt)
    return pl.pallas_call(
        _peer_route_kernel,
        grid=(n // tn,),
        in_specs=[
            pl.BlockSpec((d, tn), lambda j: (0, j)),
            pl.BlockSpec(wq_t.shape, lambda j: (0, 0)),
            pl.BlockSpec(keys_bf.shape, lambda j: (0, 0, 0)),
        ],
        out_specs=[spec] * 4,
        out_shape=[sheet(U32), sheet(U32), sheet(F32), sheet(F32)],
        scratch_shapes=[pltpu.VMEM((2, rows, tn), F32)],
        compiler_params=_params("parallel"),
        name="peer_route",
    )(h_bf, wq_t, keys_bf)


PEER_LANES = 256


def _peer_dense_kernel(h_ref, u_ref, vt_ref, lrow_ref, e1_ref, rank2_ref, e2_ref, o_ref,
                       r2_s, e2_s, w_s, acc_s, *, i1_tile):
    e = pl.program_id(1)
    tn = h_ref.shape[1]
    slabs = N_KEYS // BF16_ROWS

    @pl.when(e == 0)
    def _():
        acc_s[...] = jnp.zeros(acc_s.shape, F32)
        for h in range(P_HEADS):
            for tile in range(tn // LANE):
                rows, lanes = slice(h * N_KEYS, (h + 1) * N_KEYS), slice(tile * LANE, (tile + 1) * LANE)
                r2_s[h * (tn // LANE) + tile] = rank2_ref[rows, lanes].astype(BF16).reshape(slabs, BF16_ROWS, LANE)
                e2_s[h * (tn // LANE) + tile] = e2_ref[rows, lanes].astype(BF16).reshape(slabs, BF16_ROWS, LANE)

    zero = jnp.zeros((), BF16)
    n_groups = tn // PEER_LANES
    per_group = PEER_LANES // LANE

    def pre_act(g):
        return jnp.dot(u_ref[...], h_ref[:, g * PEER_LANES:(g + 1) * PEER_LANES],
                       preferred_element_type=F32)

    x_next = pre_act(0)
    for g in range(n_groups):
        x = x_next
        if g + 1 < n_groups:
            x_next = pre_act(g + 1)
        for il in range(i1_tile):
            for lt in range(per_group):
                tile = g * per_group + lt
                lanes = slice(tile * LANE, (tile + 1) * LANE)
                gate = None
                for h in range(P_HEADS):
                    row = il * P_HEADS + h
                    lb = pltpu.bitcast(jnp.broadcast_to(lrow_ref[row:row + 1, lanes], (SUBLANE, LANE)), BF16)
                    eb = pltpu.bitcast(jnp.broadcast_to(e1_ref[row:row + 1, lanes], (SUBLANE, LANE)), BF16)
                    slot = h * (tn // LANE) + tile
                    term = jnp.where(r2_s[slot] < lb, e2_s[slot], zero) * eb
                    gate = term if gate is None else gate + term
                act = _gelu_tanh(x[il * N_KEYS:(il + 1) * N_KEYS, lt * LANE:(lt + 1) * LANE].astype(BF16))
                w = gate * act.reshape(slabs, BF16_ROWS, LANE)
                w_s[tile, il * N_KEYS:(il + 1) * N_KEYS, :] = w.reshape(N_KEYS, LANE)
        w_group = jnp.concatenate([w_s[g * per_group + lt] for lt in range(per_group)], axis=1)
        acc_s[g] += jnp.dot(vt_ref[...], w_group, preferred_element_type=F32)

    @pl.when(e == pl.num_programs(1) - 1)
    def _():
        for g in range(n_groups):
            o_ref[:, g * PEER_LANES:(g + 1) * PEER_LANES] = acc_s[g]


def peer_dense(h_bf, u_bf, vt_bf, lrow, e1, rank2, e2, tn=TOKEN_TILE, i1_tile=16):
    d, n = h_bf.shape
    n_exp = u_bf.shape[0]
    tn = min(tn, n)
    et = i1_tile * N_KEYS
    slabs = N_KEYS // BF16_ROWS
    n_tiles = tn // LANE
    table = pl.BlockSpec((P_HEADS * N_KEYS, tn), lambda j, e: (0, j))
    return pl.pallas_call(
        functools.partial(_peer_dense_kernel, i1_tile=i1_tile),
        grid=(n // tn, n_exp // et),
        in_specs=[
            pl.BlockSpec((d, tn), lambda j, e: (0, j)),
            pl.BlockSpec((et, d), lambda j, e: (e, 0)),
            pl.BlockSpec((d, et), lambda j, e: (0, e)),
            pl.BlockSpec((i1_tile * P_HEADS, tn), lambda j, e: (e, j)),
            pl.BlockSpec((i1_tile * P_HEADS, tn), lambda j, e: (e, j)),
            table, table,
        ],
        out_specs=pl.BlockSpec((d, tn), lambda j, e: (0, j)),
        out_shape=jax.ShapeDtypeStruct((d, n), F32),
        scratch_shapes=[
            pltpu.VMEM((P_HEADS * n_tiles, slabs, BF16_ROWS, LANE), BF16),
            pltpu.VMEM((P_HEADS * n_tiles, slabs, BF16_ROWS, LANE), BF16),
            pltpu.VMEM((n_tiles, et, LANE), BF16),
            pltpu.VMEM((tn // PEER_LANES, d, PEER_LANES), F32),
        ],
        compiler_params=_params("parallel", "arbitrary"),
        name="peer_dense",
    )(h_bf, u_bf, vt_bf, lrow, e1, rank2, e2)


def peer_fm(h_bf, wq_t, keys_bf, u_bf, vt_bf):
    lrow, e1, rank2, e2 = peer_route(h_bf, wq_t, keys_bf)
    rows, n = N_KEYS * P_HEADS, h_bf.shape[1]
    head_major = lambda t: jnp.transpose(t, (1, 0, 2)).reshape(rows, n)
    return peer_dense(h_bf, u_bf, vt_bf, lrow.reshape(rows, n), e1.reshape(rows, n),
                      head_major(rank2), head_major(e2))


def _rope_tables_fm(rows):
    row = jnp.repeat(jnp.arange(rows, dtype=F32), GRID_W)
    col = jnp.tile(jnp.arange(GRID_W, dtype=F32), rows)
    inv = ROPE_BASE ** (-jnp.arange(ROPE_AXIS_PAIRS, dtype=F32) / ROPE_AXIS_PAIRS)
    ang = jnp.concatenate([inv[:, None] * row[None, :], inv[:, None] * col[None, :]], axis=0)
    return jnp.cos(ang), jnp.sin(ang)


def _kv_chunks(kc, kx, vc, vx, n_batch, tk):
    rows = kx.shape[0]
    nh = rows // V_DIM

    def per_batch(c_part, x_part):
        parts = [x_part.reshape(nh, V_DIM, n_batch, -1)]
        if c_part is not None:
            parts = [c_part.reshape(nh, V_DIM, n_batch, -1)] + parts
        return jnp.concatenate(parts, axis=-1)

    k = per_batch(kc, kx)
    v = per_batch(vc, vx)
    l = k.shape[-1]
    k = jnp.transpose(k, (2, 0, 3, 1)).reshape(n_batch, nh, l // tk, tk, V_DIM)
    extra = jnp.zeros((nh, BF16_ROWS, n_batch, l), v.dtype).at[:, 0].set(1.0)
    v = jnp.concatenate([v, extra], axis=1)
    v = jnp.transpose(v.reshape(nh, V_ROWS, n_batch, l // tk, tk), (2, 0, 3, 1, 4))
    return k, v


def _pick_tk(l):
    for tk in (768, 512, 256, 128):
        if l % tk == 0:
            return tk
    raise ValueError(f"unsupported key length {l}")


def kernel(x, c, ctx, c_ctx, ada_w, ada_b, ln_g, ln_b, ab_w_in, ab_w_out, diff_lam, diff_norm_g,
           sgu_ln_g, sgu_ln_b, sgu_w, sgu_b, pool_w_in, pool_w_grp, pool_scale, pool_w_out,
           peer_wq, peer_keys, peer_u, peer_v):
    bsz, s_len, d = x.shape
    c_len = ctx.shape[1]
    depth = ada_w.shape[0]
    qk_w = ab_w_in.shape[2] // 4
    alpha = (2.0 * depth) ** 0.25
    last_ctx_read = 2 * ((depth - 1) // 2)
    cos, sin = _rope_tables_fm(s_len // GRID_W)
    c_tile = min(TOKEN_TILE, bsz * c_len)
    cos_c = jnp.ones((HEAD_DIM // 2, c_tile), F32)
    sin_c = jnp.zeros((HEAD_DIM // 2, c_tile), F32)

    xs = x.reshape(bsz * s_len, d).T
    cs = ctx.reshape(bsz * c_len, d).T

    cond = jnp.zeros((d, LANE), F32)
    cond = cond.at[:, :bsz].set(jax.nn.silu(c).T).at[:, bsz].set(jax.nn.silu(c_ctx))
    cond = cond.astype(BF16)
    mods_x, mods_c = [], []
    for i in range(depth):
        ada = matmul_fm([(ada_w[i].T.astype(BF16), cond)], F32, bias=ada_b[i][:, None])
        mods_x.append([ada[k * d:(k + 1) * d, :bsz] for k in range(6)])
        mods_c.append([ada[k * d:(k + 1) * d, bsz:bsz + 1] for k in range(6)])

    def uses_ctx(i):
        return i < depth and (i < last_ctx_read or i % 2 == 0)

    _, hx = ln_mod_fm(xs, mod=(mods_x[0][0], mods_x[0][1]))
    hc = ln_mod_fm(cs, mod=(mods_c[0][0], mods_c[0][1]))[1] if uses_ctx(0) else None

    for i in range(depth):
        j = i // 2
        even = i % 2 == 0
        ctx_out = i < last_ctx_read
        last = i == depth - 1
        sh1, sc1, g1, sh2, sc2, g2 = mods_x[i]
        csh1, csc1, cg1, csh2, csc2, cg2 = mods_c[i]

        if even:
            lam_init = 0.8 - 0.6 * math.exp(-0.3 * i)
            lq1, lk1, lq2, lk2 = diff_lam[j].astype(F32)
            lam = (jnp.exp(jnp.sum(lq1 * lk1)) - jnp.exp(jnp.sum(lq2 * lk2)) + lam_init).reshape(1, 1)
            w_in_t = ab_w_in[j].T.astype(BF16)
            w_out_t = ab_w_out[j].T.astype(BF16)
            norm_g = diff_norm_g[j][:, None]
            sgu_args = (sgu_ln_g[j][:, None], sgu_ln_b[j][:, None],
                        jnp.transpose(sgu_w[j], (0, 2, 1)).astype(BF16), sgu_b[j][:, None, :])
            q_x, k_x, v_x, g_x = ab_in_proj(hx, w_in_t, cos, sin, *sgu_args)
            q_c, k_c, v_c, g_c = ab_in_proj(hc, w_in_t, cos_c, sin_c, *sgu_args)
            k_all, v_all = _kv_chunks(k_c, k_x, v_c, v_x, bsz, _pick_tk(c_len + s_len))
            a_x = diff_attention(q_x, k_all, v_all, lam, norm_g, lam_init, bsz, tq=min(512, s_len))
            yx = matmul_fm([(w_out_t[:, :qk_w], a_x), (w_out_t[:, qk_w:], g_x)], F32)
            if ctx_out:
                kc_only, vc_only = _kv_chunks(None, k_c, None, v_c, bsz, _pick_tk(c_len))
                a_c = diff_attention(q_c, kc_only, vc_only, lam, norm_g, lam_init, bsz, tq=c_len)
                yc = matmul_fm([(w_out_t[:, :qk_w], a_c), (w_out_t[:, qk_w:], g_c)], F32)
        else:
            p_in_t = pool_w_in[j].T.astype(BF16)
            wg_t = jnp.transpose(pool_w_grp[j], (0, 2, 1)).astype(BF16)
            p_out_t = pool_w_out[j].T.astype(BF16)
            scale = pool_scale[j][:, None]
            yx = pool_mixer_fm(matmul_fm([(p_in_t, hx)], F32), wg_t, scale, p_out_t, s_len)
            if ctx_out:
                yc = pool_mixer_fm(matmul_fm([(p_in_t, hc)], F32), wg_t, scale, p_out_t, c_len)

        wq_t = peer_wq[i].T.astype(BF16)
        keys_bf = peer_keys[i].reshape(2 * P_HEADS, N_KEYS, -1).astype(BF16)
        u_bf = peer_u[i].astype(BF16)
        vt_bf = peer_v[i].T.astype(BF16)
        next_mod_c = (mods_c[i + 1][0], mods_c[i + 1][1]) if uses_ctx(i + 1) else None
        if ctx_out:
            cs, hc = ln_mod_fm(cs, resid=(yc, cg1, ln_g[i, 0], ln_b[i, 0]), mod=(csh2, csc2), alpha=alpha)
            fc = peer_fm(hc, wq_t, keys_bf, u_bf, vt_bf)
            if uses_ctx(i + 1):
                cs, hc = ln_mod_fm(cs, resid=(fc, cg2, ln_g[i, 1], ln_b[i, 1]), mod=next_mod_c, alpha=alpha)
        elif next_mod_c is not None:
            hc = ln_mod_fm(cs, mod=next_mod_c)[1]

        xs, hx = ln_mod_fm(xs, resid=(yx, g1, ln_g[i, 0], ln_b[i, 0]), mod=(sh2, sc2), alpha=alpha)
        fx = peer_fm(hx, wq_t, keys_bf, u_bf, vt_bf)
        next_mod_x = None if last else (mods_x[i + 1][0], mods_x[i + 1][1])
        xs, hx = ln_mod_fm(xs, resid=(fx, g2, ln_g[i, 1], ln_b[i, 1]), mod=next_mod_x, alpha=alpha)

    return xs.T.reshape(bsz, s_len, d)
```

```python
import functools
import math

import jax
import jax.numpy as jnp
from jax import lax
from jax.experimental import pallas as pl
from jax.experimental.pallas import tpu as pltpu

F32 = jnp.float32
BF16 = jnp.bfloat16
U32 = jnp.uint32

GRID_W = 64
LN_EPS = 1e-5
HEAD_DIM = 64
V_DIM = 2 * HEAD_DIM
ATTN_SCALE = HEAD_DIM ** -0.5
Q_SCALE_LOG2 = ATTN_SCALE * math.log2(math.e)
ROPE_BASE = 10000.0
ROPE_AXIS_PAIRS = HEAD_DIM // 4
CHUNK = 128
SGU_GROUPS = 4
POOL_WINDOWS = (2, 4, 8, 16)
P_HEADS = 8
N_KEYS = 128
P_TOPK = 16

LANE = 128
SUBLANE = 8
BF16_ROWS = 16
VMEM_LIMIT = 56 * 1024 * 1024
TOKEN_TILE = 512


def _params(*sem):
    return pltpu.CompilerParams(dimension_semantics=sem, vmem_limit_bytes=VMEM_LIMIT)


def _gelu_tanh(x):
    c1 = math.sqrt(2.0 / math.pi)
    c2 = c1 * 0.044715
    inner = x * (c1 + c2 * (x * x))
    return x * (0.5 + 0.5 * jnp.tanh(inner))


def _mm_kernel(*refs, n_pairs, has_bias):
    o_ref = refs[-1]
    acc = None
    for i in range(n_pairs):
        part = jnp.dot(refs[2 * i][...], refs[2 * i + 1][...], preferred_element_type=F32)
        acc = part if acc is None else acc + part
    if has_bias:
        acc = acc + refs[2 * n_pairs][...]
    o_ref[...] = acc.astype(o_ref.dtype)


def matmul_fm(pairs, out_dtype, bias=None, tn=TOKEN_TILE, tm=None):
    m = pairs[0][0].shape[0]
    n = pairs[0][1].shape[1]
    tn = min(tn, n)
    tm = m if tm is None else tm
    args, specs = [], []
    for w, x in pairs:
        k = w.shape[1]
        args += [w, x]
        specs += [pl.BlockSpec((tm, k), lambda j, i: (i, 0)),
                  pl.BlockSpec((k, tn), lambda j, i: (0, j))]
    if bias is not None:
        args.append(bias)
        specs.append(pl.BlockSpec((tm, 1), lambda j, i: (i, 0)))
    return pl.pallas_call(
        functools.partial(_mm_kernel, n_pairs=len(pairs), has_bias=bias is not None),
        grid=(n // tn, m // tm),
        in_specs=specs,
        out_specs=pl.BlockSpec((tm, tn), lambda j, i: (i, j)),
        out_shape=jax.ShapeDtypeStruct((m, n), out_dtype),
        compiler_params=_params("parallel", "parallel"),
        name="matmul_fm",
    )(*args)


def _ln_mod_kernel(*refs, alpha, has_resid, emit_x, emit_h):
    it = iter(refs)
    x = next(it)[...]
    if has_resid:
        y_ref, gate_ref, lng_ref, lnb_ref = next(it), next(it), next(it), next(it)
    if emit_h:
        shift_ref, scale_ref = next(it), next(it)
    if has_resid:
        z = alpha * x + gate_ref[...] * y_ref[...]
        mu = jnp.mean(z, axis=0, keepdims=True)
        zc = z - mu
        var = jnp.mean(zc * zc, axis=0, keepdims=True)
        x = zc * lax.rsqrt(var + LN_EPS) * lng_ref[...] + lnb_ref[...]
        if emit_x:
            next(it)[...] = x
    if emit_h:
        next(it)[...] = (x * (1.0 + scale_ref[...]) + shift_ref[...]).astype(BF16)


def ln_mod_fm(xs, *, resid=None, mod=None, alpha=1.0, emit_x=True, tn=TOKEN_TILE):
    d, n = xs.shape
    tn = min(tn, n)
    n_seg = (resid[1] if resid is not None else mod[0]).shape[1]
    tps = (n // n_seg) // tn
    tok = pl.BlockSpec((d, tn), lambda j: (0, j))
    seg = pl.BlockSpec((None, d, 1), lambda j: (j // tps, 0, 0))
    col = pl.BlockSpec((d, 1), lambda j: (0, 0))
    args, specs, out_shape, out_specs = [xs], [tok], [], []
    if resid is not None:
        y, gate, ln_g, ln_b = resid
        args += [y, gate.T[:, :, None], ln_g[:, None], ln_b[:, None]]
        specs += [tok, seg, col, col]
        if emit_x:
            out_shape.append(jax.ShapeDtypeStruct((d, n), F32))
            out_specs.append(tok)
    if mod is not None:
        args += [mod[0].T[:, :, None], mod[1].T[:, :, None]]
        specs += [seg, seg]
        out_shape.append(jax.ShapeDtypeStruct((d, n), BF16))
        out_specs.append(tok)
    outs = pl.pallas_call(
        functools.partial(_ln_mod_kernel, alpha=alpha, has_resid=resid is not None,
                          emit_x=emit_x and resid is not None, emit_h=mod is not None),
        grid=(n // tn,),
        in_specs=specs,
        out_specs=out_specs,
        out_shape=out_shape,
        compiler_params=_params("parallel"),
        name="ln_mod",
    )(*args)
    outs = list(outs)
    x_new = outs.pop(0) if (resid is not None and emit_x) else None
    h = outs.pop(0) if mod is not None else None
    return x_new, h


ATTN_Q_TILE = 1024
V_ROWS = V_DIM + BF16_ROWS


def _attn_kernel(lam_ref, q_ref, k_ref, v_ref, g_ref, o_ref, qp_ref, m_ref, acc_ref,
                 *, tq, n_chunks, out_scale):
    q = q_ref[...]
    row = lax.broadcasted_iota(jnp.int32, q.shape, 0)
    zero = jnp.zeros_like(q)
    qp_ref[:, :tq] = jnp.where(row < HEAD_DIM, q, zero)
    qp_ref[:, tq:] = jnp.where(row >= HEAD_DIM, q, zero)
    m_ref[...] = jnp.full(m_ref.shape, -jnp.inf, F32)
    acc_ref[...] = jnp.zeros(acc_ref.shape, F32)

    n_groups = 2
    items = [(c, g) for c in range(n_chunks) for g in range(n_groups)]

    def lanes_of(item):
        return slice(item[1] * tq, (item[1] + 1) * tq)

    def scores(item):
        return jnp.dot(k_ref[item[0]], qp_ref[:, lanes_of(item)], preferred_element_type=F32)

    def running_max(item, s):
        m_prev = m_ref[:, lanes_of(item)]
        m_new = jnp.maximum(m_prev, jnp.max(s, axis=0, keepdims=True))
        m_ref[:, lanes_of(item)] = m_new
        return m_new, jnp.exp2(m_prev - m_new)

    assert n_groups >= 2
    n_items = len(items)
    s_buf = {0: scores(items[0])}
    if n_items > 1:
        s_buf[1] = scores(items[1])
    stats = {0: running_max(items[0], s_buf[0])}
    for idx, item in enumerate(items):
        if idx + 2 < n_items:
            s_buf[idx + 2] = scores(items[idx + 2])
        if idx + 1 < n_items:
            stats[idx + 1] = running_max(items[idx + 1], s_buf[idx + 1])
        m_new, alpha = stats.pop(idx)
        p = jnp.exp2(s_buf.pop(idx) - m_new).astype(BF16)
        lanes = lanes_of(item)
        acc_ref[:, lanes] = acc_ref[:, lanes] * alpha + jnp.dot(v_ref[item[0]], p, preferred_element_type=F32)

    lam = lam_ref[0, 0]
    acc = acc_ref[...]
    inv = 1.0 / acc[V_DIM:V_DIM + 1, :]
    o = acc[:V_DIM, :tq] * inv[:, :tq] - lam * (acc[:V_DIM, tq:] * inv[:, tq:])
    ms = jnp.mean(o * o, axis=0, keepdims=True)
    o = o * lax.rsqrt(ms + LN_EPS) * g_ref[...] * out_scale
    o_ref[...] = o.astype(o_ref.dtype)


def diff_attention(q_fm, k_chunks, v_chunks, lam, norm_g, lam_init, n_batch, tq):
    n_heads = q_fm.shape[0] // V_DIM
    sq = q_fm.shape[1] // n_batch
    nq = sq // tq
    _, _, nc, tk, _ = k_chunks.shape
    return pl.pallas_call(
        functools.partial(_attn_kernel, tq=tq, n_chunks=nc, out_scale=1.0 - lam_init),
        grid=(n_batch, n_heads, nq),
        in_specs=[
            pl.BlockSpec(memory_space=pltpu.SMEM),
            pl.BlockSpec((V_DIM, tq), lambda b, h, i: (h, b * nq + i)),
            pl.BlockSpec((None, None, nc, tk, V_DIM), lambda b, h, i: (b, h, 0, 0, 0)),
            pl.BlockSpec((None, None, nc, V_ROWS, tk), lambda b, h, i: (b, h, 0, 0, 0)),
            pl.BlockSpec((V_DIM, 1), lambda b, h, i: (0, 0)),
        ],
        out_specs=pl.BlockSpec((V_DIM, tq), lambda b, h, i: (h, b * nq + i)),
        out_shape=jax.ShapeDtypeStruct(q_fm.shape, BF16),
        scratch_shapes=[
            pltpu.VMEM((V_DIM, 2 * tq), BF16),
            pltpu.VMEM((1, 2 * tq), F32),
            pltpu.VMEM((V_ROWS, 2 * tq), F32),
        ],
        compiler_params=_params("parallel", "parallel", "parallel"),
        name="diff_attention",
    )(lam, q_fm, k_chunks, v_chunks, norm_g)


def _ab_in_kernel(h_ref, w_ref, cos_ref, sin_ref, lng_ref, lnb_ref, wst_ref, bs_ref,
                  q_ref, k_ref, v_ref, g_ref, *, qk_w, sgu_half):
    p = jnp.dot(w_ref[...], h_ref[...], preferred_element_type=F32)
    cs, sn = cos_ref[...], sin_ref[...]
    half = HEAD_DIM // 2
    for blk in range(qk_w // HEAD_DIM):
        r = blk * HEAD_DIM
        a, b = p[r:r + half], p[r + half:r + HEAD_DIM]
        q_ref[r:r + half, :] = ((a * cs - b * sn) * Q_SCALE_LOG2).astype(BF16)
        q_ref[r + half:r + HEAD_DIM, :] = ((a * sn + b * cs) * Q_SCALE_LOG2).astype(BF16)
        a, b = p[qk_w + r:qk_w + r + half], p[qk_w + r + half:qk_w + r + HEAD_DIM]
        k_ref[r:r + half, :] = (a * cs - b * sn).astype(BF16)
        k_ref[r + half:r + HEAD_DIM, :] = (a * sn + b * cs).astype(BF16)
    v_ref[...] = p[2 * qk_w:3 * qk_w].astype(BF16)

    gu = p[3 * qk_w:3 * qk_w + sgu_half]
    v = _gelu_tanh(p[3 * qk_w + sgu_half:])
    mu = jnp.mean(v, axis=0, keepdims=True)
    vc = v - mu
    var = jnp.mean(vc * vc, axis=0, keepdims=True)
    vb = (vc * lax.rsqrt(var + LN_EPS) * lng_ref[...] + lnb_ref[...]).astype(BF16)
    gc = sgu_half // SGU_GROUPS
    for g in range(SGU_GROUPS):
        for j in range(vb.shape[1] // CHUNK):
            rows, cols = slice(g * gc, (g + 1) * gc), slice(j * CHUNK, (j + 1) * CHUNK)
            s = jnp.dot(vb[rows, cols], wst_ref[g], preferred_element_type=F32) + bs_ref[g]
            g_ref[rows, cols] = (_gelu_tanh(gu[rows, cols]) * s).astype(BF16)


def ab_in_proj(h_bf, w_in_t, cos_tab, sin_tab, ln_g, ln_b, ws_t, bs, tn=TOKEN_TILE):
    d, n = h_bf.shape
    tn = min(tn, n)
    qk_w = w_in_t.shape[0] // 4
    sgu_half = (w_in_t.shape[0] - 3 * qk_w) // 2
    n_pos = cos_tab.shape[1] // tn
    tok = lambda rows: pl.BlockSpec((rows, tn), lambda j: (0, j))
    pos = pl.BlockSpec((HEAD_DIM // 2, tn), lambda j: (0, j % n_pos))
    return pl.pallas_call(
        functools.partial(_ab_in_kernel, qk_w=qk_w, sgu_half=sgu_half),
        grid=(n // tn,),
        in_specs=[
            tok(d),
            pl.BlockSpec(w_in_t.shape, lambda j: (0, 0)),
            pos, pos,
            pl.BlockSpec((sgu_half, 1), lambda j: (0, 0)),
            pl.BlockSpec((sgu_half, 1), lambda j: (0, 0)),
            pl.BlockSpec((SGU_GROUPS, CHUNK, CHUNK), lambda j: (0, 0, 0)),
            pl.BlockSpec((SGU_GROUPS, 1, CHUNK), lambda j: (0, 0, 0)),
        ],
        out_specs=[tok(qk_w), tok(qk_w), tok(qk_w), tok(sgu_half)],
        out_shape=[jax.ShapeDtypeStruct((qk_w, n), BF16)] * 3 + [jax.ShapeDtypeStruct((sgu_half, n), BF16)],
        compiler_params=_params("parallel"),
        name="ab_in_proj",
    )(h_bf, w_in_t, cos_tab, sin_tab, ln_g, ln_b, ws_t, bs)


POOL_TILE = 256
POOL_HALO = 128


def _pool_kernel(left_ref, mid_ref, right_ref, wg_ref, scale_ref, wout_ref, o_ref, *, seq_len):
    j = pl.program_id(0)
    t0 = j * POOL_TILE
    seq_start = (t0 // seq_len) * seq_len
    win = POOL_TILE + 2 * POOL_HALO
    h_all = jnp.concatenate([left_ref[...], mid_ref[...], right_ref[...]], axis=1)
    h_hi = h_all.astype(BF16)
    h_lo = (h_all - h_hi.astype(F32)).astype(BF16)
    tau = t0 - POOL_HALO + lax.broadcasted_iota(jnp.int32, (win, POOL_TILE), 0)
    t = t0 + lax.broadcasted_iota(jnp.int32, (win, POOL_TILE), 1)
    gc = mid_ref.shape[0] // len(POOL_WINDOWS)
    outs = []
    for g, w in enumerate(POOL_WINDOWS):
        lo = jnp.maximum(t - w // 2, seq_start)
        hi = jnp.minimum(t + (w - w // 2), seq_start + seq_len)
        band = jnp.where((tau >= lo) & (tau < hi), 1.0, 0.0).astype(BF16)
        cnt = (hi - lo)[0:1, :].astype(F32)
        sl = slice(g * gc, (g + 1) * gc)
        ssum = (jnp.dot(h_hi[sl], band, preferred_element_type=F32)
                + jnp.dot(h_lo[sl], band, preferred_element_type=F32))
        m = ssum / cnt - mid_ref[sl, :]
        m = jnp.dot(wg_ref[g], m.astype(BF16), preferred_element_type=F32)
        outs.append(m)
    m_all = (jnp.concatenate(outs, axis=0) * scale_ref[...]).astype(BF16)
    o_ref[...] = jnp.dot(wout_ref[...], m_all, preferred_element_type=F32)


def pool_mixer_fm(h, wg_t, scale, wout_t, seq_len):
    c, n = h.shape
    nt = n // POOL_TILE
    r = POOL_TILE // POOL_HALO
    nh = n // POOL_HALO
    return pl.pallas_call(
        functools.partial(_pool_kernel, seq_len=seq_len),
        grid=(nt,),
        in_specs=[
            pl.BlockSpec((c, POOL_HALO), lambda j: (0, jnp.maximum(j * r - 1, 0))),
            pl.BlockSpec((c, POOL_TILE), lambda j: (0, j)),
            pl.BlockSpec((c, POOL_HALO), lambda j: (0, jnp.minimum(j * r + r, nh - 1))),
            pl.BlockSpec(wg_t.shape, lambda j: (0, 0, 0)),
            pl.BlockSpec((c, 1), lambda j: (0, 0)),
            pl.BlockSpec(wout_t.shape, lambda j: (0, 0)),
        ],
        out_specs=pl.BlockSpec((wout_t.shape[0], POOL_TILE), lambda j: (0, j)),
        out_shape=jax.ShapeDtypeStruct((wout_t.shape[0], n), F32),
        compiler_params=_params("parallel"),
        name="pool_mixer",
    )(h, h, h, wg_t, scale, wout_t)


def _sort16_desc(xs):
    xs = list(xs)
    n = len(xs)
    k = 2
    while k <= n:
        j = k // 2
        while j >= 1:
            for i in range(n):
                l = i ^ j
                if l > i:
                    hi, lo = jnp.maximum(xs[i], xs[l]), jnp.minimum(xs[i], xs[l])
                    if (i & k) == 0:
                        xs[i], xs[l] = hi, lo
                    else:
                        xs[i], xs[l] = lo, hi
            j //= 2
        k *= 2
    return xs


def _bitonic_merge_desc(xs):
    xs = list(xs)
    n = len(xs)
    j = n // 2
    while j >= 1:
        for i in range(n):
            l = i ^ j
            if l > i:
                xs[i], xs[l] = jnp.maximum(xs[i], xs[l]), jnp.minimum(xs[i], xs[l])
        j //= 2
    return xs


def _top16_of_two(a, b):
    n = len(a)
    return _bitonic_merge_desc([jnp.maximum(a[i], b[n - 1 - i]) for i in range(n)])


def _top16_desc(vals):
    groups = [_sort16_desc(vals[i:i + P_TOPK]) for i in range(0, len(vals), P_TOPK)]
    while len(groups) > 1:
        groups = [_top16_of_two(groups[i], groups[i + 1]) for i in range(0, len(groups), 2)]
    return groups[0]


def _peer_route_kernel(h_ref, wq_ref, keys_ref, lrow_ref, e1_ref, rank2_ref, e2_ref, s_s):
    q = jnp.dot(wq_ref[...], h_ref[...], preferred_element_type=F32).astype(BF16)
    for hp in range(2 * P_HEADS):
        h, p = divmod(hp, 2)
        s = jnp.dot(keys_ref[hp], q[hp * N_KEYS:(hp + 1) * N_KEYS, :], preferred_element_type=F32)
        s_s[p, pl.ds(h, N_KEYS, stride=P_HEADS), :] = s

    s1 = [s_s[0, k * P_HEADS:(k + 1) * P_HEADS, :] for k in range(N_KEYS)]
    s2 = [s_s[1, k * P_HEADS:(k + 1) * P_HEADS, :] for k in range(N_KEYS)]
    a = _top16_desc(s1)
    b = _top16_desc(s2)
    cand = [[a[r1] + b[r2] for r2 in range(P_TOPK)] for r1 in range(P_TOPK)]
    rows = [list(r) for r in cand]
    while len(rows) > 1:
        rows = [_top16_of_two(rows[i], rows[i + 1]) for i in range(0, len(rows), 2)]
    best = rows[0]
    thr = best[P_TOPK - 1]
    z = None
    for r in range(P_TOPK):
        e = jnp.exp(best[r] - best[0])
        z = e if z is None else z + e
    inv_z = 1.0 / z
    counts = []
    for r1 in range(P_TOPK):
        cnt = jnp.zeros_like(thr)
        for r2 in range(P_TOPK):
            cnt = cnt + jnp.where(cand[r1][r2] >= thr, 1.0, 0.0)
        counts.append(cnt)

    def per_key(k, carry):
        base = pl.multiple_of(k * P_HEADS, P_HEADS)
        s1k = s_s[0, pl.ds(base, P_HEADS), :]
        s2k = s_s[1, pl.ds(base, P_HEADS), :]
        lrow = jnp.zeros_like(thr)
        rank = jnp.zeros_like(thr)
        for r in range(P_TOPK):
            lrow = jnp.where(s1k == a[r], counts[r], lrow)
            rank = jnp.where(b[r] > s2k, float(r + 1), rank)
        lrow_ref[k] = lrow
        e1_ref[k] = jnp.exp(s1k - a[0])
        rank2_ref[k] = rank
        e2_ref[k] = jnp.exp(s2k - b[0]) * inv_z
        return carry

    lax.fori_loop(0, N_KEYS, per_key, 0, unroll=4)


def peer_route(h_bf, wq_t, keys_bf, tn=LANE):
    d, n = h_bf.shape
    rows = N_KEYS * P_HEADS
    spec = pl.BlockSpec((N_KEYS, P_HEADS, tn), lambda j: (0, 0, j))
    sheet = lambda dt: jax.ShapeDtypeStruct((N_KEYS, P_HEADS, n), dt)
    return pl.pallas_call(
        _peer_route_kernel,
        grid=(n // tn,),
        in_specs=[
            pl.BlockSpec((d, tn), lambda j: (0, j)),
            pl.BlockSpec(wq_t.shape, lambda j: (0, 0)),
            pl.BlockSpec(keys_bf.shape, lambda j: (0, 0, 0)),
        ],
        out_specs=[spec] * 4,
        out_shape=[sheet(F32)] * 4,
        scratch_shapes=[pltpu.VMEM((2, rows, tn), F32)],
        compiler_params=_params("parallel"),
        name="peer_route",
    )(h_bf, wq_t, keys_bf)


PEER_TOKEN_TILE = 1024
PEER_I1_TILE = 8


def _peer_dense_kernel(h_ref, u_ref, vt_ref, lrow_ref, e1_ref, rank2_ref, e2_ref, o_ref,
                       r2_s, e2_s, w_s, *, i1_tile):
    e = pl.program_id(1)

    @pl.when(e == 0)
    def _():
        o_ref[...] = jnp.zeros(o_ref.shape, F32)
        for h in range(P_HEADS):
            r2_s[h] = rank2_ref[h * N_KEYS:(h + 1) * N_KEYS, :].astype(BF16)
            e2_s[h] = e2_ref[h * N_KEYS:(h + 1) * N_KEYS, :].astype(BF16)

    act = _gelu_tanh(jnp.dot(u_ref[...], h_ref[...], preferred_element_type=F32).astype(BF16))
    tn = act.shape[1]
    zero = jnp.zeros((), BF16)
    for il in range(i1_tile):
        lr = lrow_ref[il * P_HEADS:(il + 1) * P_HEADS, :]
        e1 = e1_ref[il * P_HEADS:(il + 1) * P_HEADS, :]
        gate = None
        for h in range(P_HEADS):
            lb = jnp.broadcast_to(lr[h:h + 1, :], (N_KEYS, tn)).astype(BF16)
            eb = jnp.broadcast_to(e1[h:h + 1, :], (N_KEYS, tn)).astype(BF16)
            term = jnp.where(r2_s[h] < lb, e2_s[h], zero) * eb
            gate = term if gate is None else gate + term
        w_s[il * N_KEYS:(il + 1) * N_KEYS, :] = gate * act[il * N_KEYS:(il + 1) * N_KEYS, :]
    o_ref[...] += jnp.dot(vt_ref[...], w_s[...], preferred_element_type=F32)


def peer_dense(h_bf, u_bf, vt_bf, layer, lrow, e1, rank2, e2, tn=PEER_TOKEN_TILE, i1_tile=PEER_I1_TILE):
    d, n = h_bf.shape
    n_exp = u_bf.shape[1]
    tn = min(tn, n)
    et = i1_tile * N_KEYS
    table = pl.BlockSpec((P_HEADS * N_KEYS, tn), lambda j, e: (0, j))
    return pl.pallas_call(
        functools.partial(_peer_dense_kernel, i1_tile=i1_tile),
        grid=(n // tn, n_exp // et),
        in_specs=[
            pl.BlockSpec((d, tn), lambda j, e: (0, j)),
            pl.BlockSpec((None, et, d), lambda j, e: (layer, e, 0)),
            pl.BlockSpec((None, d, et), lambda j, e: (layer, 0, e)),
            pl.BlockSpec((i1_tile * P_HEADS, tn), lambda j, e: (e, j)),
            pl.BlockSpec((i1_tile * P_HEADS, tn), lambda j, e: (e, j)),
            table, table,
        ],
        out_specs=pl.BlockSpec((d, tn), lambda j, e: (0, j)),
        out_shape=jax.ShapeDtypeStruct((d, n), F32),
        scratch_shapes=[
            pltpu.VMEM((P_HEADS, N_KEYS, tn), BF16),
            pltpu.VMEM((P_HEADS, N_KEYS, tn), BF16),
            pltpu.VMEM((et, tn), BF16),
        ],
        compiler_params=_params("parallel", "arbitrary"),
        name="peer_dense",
    )(h_bf, u_bf, vt_bf, lrow, e1, rank2, e2)


def peer_fm(h_bf, wq_t, keys_bf, u_bf, vt_bf, layer):
    lrow, e1, rank2, e2 = peer_route(h_bf, wq_t, keys_bf)
    rows, n = N_KEYS * P_HEADS, h_bf.shape[1]
    head_major = lambda t: jnp.transpose(t, (1, 0, 2)).reshape(rows, n)
    return peer_dense(h_bf, u_bf, vt_bf, layer, lrow.reshape(rows, n), e1.reshape(rows, n),
                      head_major(rank2), head_major(e2))


def _rope_tables_fm(rows):
    row = jnp.repeat(jnp.arange(rows, dtype=F32), GRID_W)
    col = jnp.tile(jnp.arange(GRID_W, dtype=F32), rows)
    inv = ROPE_BASE ** (-jnp.arange(ROPE_AXIS_PAIRS, dtype=F32) / ROPE_AXIS_PAIRS)
    ang = jnp.concatenate([inv[:, None] * row[None, :], inv[:, None] * col[None, :]], axis=0)
    return jnp.cos(ang), jnp.sin(ang)


def _kv_chunks(kc, kx, vc, vx, n_batch, tk):
    rows = kx.shape[0]
    nh = rows // V_DIM

    def per_batch(c_part, x_part):
        parts = [x_part.reshape(nh, V_DIM, n_batch, -1)]
        if c_part is not None:
            parts = [c_part.reshape(nh, V_DIM, n_batch, -1)] + parts
        return jnp.concatenate(parts, axis=-1)

    k = per_batch(kc, kx)
    v = per_batch(vc, vx)
    l = k.shape[-1]
    k = jnp.transpose(k, (2, 0, 3, 1)).reshape(n_batch, nh, l // tk, tk, V_DIM)
    extra = jnp.zeros((nh, BF16_ROWS, n_batch, l), v.dtype).at[:, 0].set(1.0)
    v = jnp.concatenate([v, extra], axis=1)
    v = jnp.transpose(v.reshape(nh, V_ROWS, n_batch, l // tk, tk), (2, 0, 3, 1, 4))
    return k, v


def _pick_tk(l):
    for tk in (768, 512, 256, 128):
        if l % tk == 0:
            return tk
    raise ValueError(f"unsupported key length {l}")


def kernel(x, c, ctx, c_ctx, ada_w, ada_b, ln_g, ln_b, ab_w_in, ab_w_out, diff_lam, diff_norm_g,
           sgu_ln_g, sgu_ln_b, sgu_w, sgu_b, pool_w_in, pool_w_grp, pool_scale, pool_w_out,
           peer_wq, peer_keys, peer_u, peer_v):
    bsz, s_len, d = x.shape
    c_len = ctx.shape[1]
    depth = ada_w.shape[0]
    qk_w = ab_w_in.shape[2] // 4
    alpha = (2.0 * depth) ** 0.25
    last_ctx_read = 2 * ((depth - 1) // 2)
    cos, sin = _rope_tables_fm(s_len // GRID_W)
    c_tile = min(TOKEN_TILE, bsz * c_len)
    cos_c = jnp.ones((HEAD_DIM // 2, c_tile), F32)
    sin_c = jnp.zeros((HEAD_DIM // 2, c_tile), F32)

    xs = x.reshape(bsz * s_len, d).T
    cs = ctx.reshape(bsz * c_len, d).T

    cond = jnp.zeros((d, LANE), F32)
    cond = cond.at[:, :bsz].set(jax.nn.silu(c).T).at[:, bsz].set(jax.nn.silu(c_ctx))
    cond = cond.astype(BF16)
    u_bf = peer_u.astype(BF16)
    vt_bf = jnp.transpose(peer_v, (0, 2, 1)).astype(BF16)
    mods_x, mods_c = [], []
    for i in range(depth):
        ada = matmul_fm([(ada_w[i].T.astype(BF16), cond)], F32, bias=ada_b[i][:, None])
        mods_x.append([ada[k * d:(k + 1) * d, :bsz] for k in range(6)])
        mods_c.append([ada[k * d:(k + 1) * d, bsz:bsz + 1] for k in range(6)])

    def uses_ctx(i):
        return i < depth and (i < last_ctx_read or i % 2 == 0)

    _, hx = ln_mod_fm(xs, mod=(mods_x[0][0], mods_x[0][1]))
    hc = ln_mod_fm(cs, mod=(mods_c[0][0], mods_c[0][1]))[1] if uses_ctx(0) else None

    for i in range(depth):
        j = i // 2
        even = i % 2 == 0
        ctx_out = i < last_ctx_read
        last = i == depth - 1
        sh1, sc1, g1, sh2, sc2, g2 = mods_x[i]
        csh1, csc1, cg1, csh2, csc2, cg2 = mods_c[i]

        if even:
            lam_init = 0.8 - 0.6 * math.exp(-0.3 * i)
            lq1, lk1, lq2, lk2 = diff_lam[j].astype(F32)
            lam = (jnp.exp(jnp.sum(lq1 * lk1)) - jnp.exp(jnp.sum(lq2 * lk2)) + lam_init).reshape(1, 1)
            w_in_t = ab_w_in[j].T.astype(BF16)
            w_out_t = ab_w_out[j].T.astype(BF16)
            norm_g = diff_norm_g[j][:, None]
            sgu_args = (sgu_ln_g[j][:, None], sgu_ln_b[j][:, None],
                        jnp.transpose(sgu_w[j], (0, 2, 1)).astype(BF16), sgu_b[j][:, None, :])
            q_x, k_x, v_x, g_x = ab_in_proj(hx, w_in_t, cos, sin, *sgu_args)
            q_c, k_c, v_c, g_c = ab_in_proj(hc, w_in_t, cos_c, sin_c, *sgu_args)
            k_all, v_all = _kv_chunks(k_c, k_x, v_c, v_x, bsz, _pick_tk(c_len + s_len))
            a_x = diff_attention(q_x, k_all, v_all, lam, norm_g, lam_init, bsz, tq=min(ATTN_Q_TILE, s_len))
            yx = matmul_fm([(w_out_t[:, :qk_w], a_x), (w_out_t[:, qk_w:], g_x)], F32)
            if ctx_out:
                kc_only, vc_only = _kv_chunks(None, k_c, None, v_c, bsz, _pick_tk(c_len))
                a_c = diff_attention(q_c, kc_only, vc_only, lam, norm_g, lam_init, bsz, tq=c_len)
                yc = matmul_fm([(w_out_t[:, :qk_w], a_c), (w_out_t[:, qk_w:], g_c)], F32)
        else:
            p_in_t = pool_w_in[j].T.astype(BF16)
            wg_t = jnp.transpose(pool_w_grp[j], (0, 2, 1)).astype(BF16)
            p_out_t = pool_w_out[j].T.astype(BF16)
            scale = pool_scale[j][:, None]
            yx = pool_mixer_fm(matmul_fm([(p_in_t, hx)], F32), wg_t, scale, p_out_t, s_len)
            if ctx_out:
                yc = pool_mixer_fm(matmul_fm([(p_in_t, hc)], F32), wg_t, scale, p_out_t, c_len)

        wq_t = peer_wq[i].T.astype(BF16)
        keys_bf = peer_keys[i].reshape(2 * P_HEADS, N_KEYS, -1).astype(BF16)
        next_mod_c = (mods_c[i + 1][0], mods_c[i + 1][1]) if uses_ctx(i + 1) else None
        if ctx_out:
            cs, hc = ln_mod_fm(cs, resid=(yc, cg1, ln_g[i, 0], ln_b[i, 0]), mod=(csh2, csc2), alpha=alpha)
            fc = peer_fm(hc, wq_t, keys_bf, u_bf, vt_bf, i)
            if uses_ctx(i + 1):
                cs, hc = ln_mod_fm(cs, resid=(fc, cg2, ln_g[i, 1], ln_b[i, 1]), mod=next_mod_c, alpha=alpha)
        elif next_mod_c is not None:
            hc = ln_mod_fm(cs, mod=next_mod_c)[1]

        xs, hx = ln_mod_fm(xs, resid=(yx, g1, ln_g[i, 0], ln_b[i, 0]), mod=(sh2, sc2), alpha=alpha)
        fx = peer_fm(hx, wq_t, keys_bf, u_bf, vt_bf, i)
        next_mod_x = None if last else (mods_x[i + 1][0], mods_x[i + 1][1])
        xs, hx = ln_mod_fm(xs, resid=(fx, g2, ln_g[i, 1], ln_b[i, 1]), mod=next_mod_x, alpha=alpha)

    return xs.T.reshape(bsz, s_len, d)
```

```python
import functools
import math

import jax
import jax.numpy as jnp
from jax import lax
from jax.experimental import pallas as pl
from jax.experimental.pallas import tpu as pltpu

F32 = jnp.float32
BF16 = jnp.bfloat16
U32 = jnp.uint32

GRID_W = 64
LN_EPS = 1e-5
HEAD_DIM = 64
V_DIM = 2 * HEAD_DIM
ATTN_SCALE = HEAD_DIM ** -0.5
Q_SCALE_LOG2 = ATTN_SCALE * math.log2(math.e)
ROPE_BASE = 10000.0
ROPE_AXIS_PAIRS = HEAD_DIM // 4
CHUNK = 128
SGU_GROUPS = 4
POOL_WINDOWS = (2, 4, 8, 16)
P_HEADS = 8
N_KEYS = 128
P_TOPK = 16

LANE = 128
SUBLANE = 8
BF16_ROWS = 16
VMEM_LIMIT = 56 * 1024 * 1024
TOKEN_TILE = 512


def _params(*sem):
    return pltpu.CompilerParams(dimension_semantics=sem, vmem_limit_bytes=VMEM_LIMIT)


def _gelu_tanh(x):
    c1 = math.sqrt(2.0 / math.pi)
    c2 = c1 * 0.044715
    inner = x * (c1 + c2 * (x * x))
    return x * (0.5 + 0.5 * jnp.tanh(inner))


def _mm_kernel(*refs, n_pairs, has_bias):
    o_ref = refs[-1]
    acc = None
    for i in range(n_pairs):
        part = jnp.dot(refs[2 * i][...].astype(BF16), refs[2 * i + 1][...].astype(BF16),
                       preferred_element_type=F32)
        acc = part if acc is None else acc + part
    if has_bias:
        acc = acc + refs[2 * n_pairs][...]
    o_ref[...] = acc.astype(o_ref.dtype)


def matmul_fm(pairs, out_dtype, bias=None, tn=TOKEN_TILE, tm=None):
    m = pairs[0][0].shape[0]
    n = pairs[0][1].shape[1]
    tn = min(tn, n)
    tm = m if tm is None else tm
    args, specs = [], []
    for w, x in pairs:
        k = w.shape[1]
        args += [w, x]
        specs += [pl.BlockSpec((tm, k), lambda j, i: (i, 0)),
                  pl.BlockSpec((k, tn), lambda j, i: (0, j))]
    if bias is not None:
        args.append(bias)
        specs.append(pl.BlockSpec((tm, 1), lambda j, i: (i, 0)))
    return pl.pallas_call(
        functools.partial(_mm_kernel, n_pairs=len(pairs), has_bias=bias is not None),
        grid=(n // tn, m // tm),
        in_specs=specs,
        out_specs=pl.BlockSpec((tm, tn), lambda j, i: (i, j)),
        out_shape=jax.ShapeDtypeStruct((m, n), out_dtype),
        compiler_params=_params("parallel", "parallel"),
        name="matmul_fm",
    )(*args)


def _ln_mod_kernel(*refs, alpha, has_resid, emit_x, emit_h):
    it = iter(refs)
    x = next(it)[...]
    if has_resid:
        y_ref, gate_ref, lng_ref, lnb_ref = next(it), next(it), next(it), next(it)
    if emit_h:
        shift_ref, scale_ref = next(it), next(it)
    if has_resid:
        z = alpha * x + gate_ref[...] * y_ref[...]
        mu = jnp.mean(z, axis=0, keepdims=True)
        zc = z - mu
        var = jnp.mean(zc * zc, axis=0, keepdims=True)
        x = zc * lax.rsqrt(var + LN_EPS) * lng_ref[...] + lnb_ref[...]
        if emit_x:
            next(it)[...] = x
    if emit_h:
        next(it)[...] = (x * (1.0 + scale_ref[...]) + shift_ref[...]).astype(BF16)


def ln_mod_fm(xs, *, resid=None, mod=None, alpha=1.0, emit_x=True, tn=TOKEN_TILE):
    d, n = xs.shape
    tn = min(tn, n)
    n_seg = (resid[1] if resid is not None else mod[0]).shape[1]
    tps = (n // n_seg) // tn
    tok = pl.BlockSpec((d, tn), lambda j: (0, j))
    seg = pl.BlockSpec((None, d, 1), lambda j: (j // tps, 0, 0))
    col = pl.BlockSpec((d, 1), lambda j: (0, 0))
    args, specs, out_shape, out_specs = [xs], [tok], [], []
    if resid is not None:
        y, gate, ln_g, ln_b = resid
        args += [y, gate.T[:, :, None], ln_g[:, None], ln_b[:, None]]
        specs += [tok, seg, col, col]
        if emit_x:
            out_shape.append(jax.ShapeDtypeStruct((d, n), F32))
            out_specs.append(tok)
    if mod is not None:
        args += [mod[0].T[:, :, None], mod[1].T[:, :, None]]
        specs += [seg, seg]
        out_shape.append(jax.ShapeDtypeStruct((d, n), BF16))
        out_specs.append(tok)
    outs = pl.pallas_call(
        functools.partial(_ln_mod_kernel, alpha=alpha, has_resid=resid is not None,
                          emit_x=emit_x and resid is not None, emit_h=mod is not None),
        grid=(n // tn,),
        in_specs=specs,
        out_specs=out_specs,
        out_shape=out_shape,
        compiler_params=_params("parallel"),
        name="ln_mod",
    )(*args)
    outs = list(outs)
    x_new = outs.pop(0) if (resid is not None and emit_x) else None
    h = outs.pop(0) if mod is not None else None
    return x_new, h


ATTN_Q_TILE = 1024
V_ROWS = V_DIM + BF16_ROWS


def _attn_kernel(lam_ref, q_ref, k_ref, v_ref, ones_ref, g_ref, o_ref, qp_ref, m_ref, acc_ref,
                 *, tq, tk, n_chunks, out_scale):
    q = q_ref[...]
    row = lax.broadcasted_iota(jnp.int32, q.shape, 0)
    zero = jnp.zeros_like(q)
    qp_ref[:, :tq] = jnp.where(row < HEAD_DIM, q, zero)
    qp_ref[:, tq:] = jnp.where(row >= HEAD_DIM, q, zero)
    m_ref[...] = jnp.full(m_ref.shape, -jnp.inf, F32)
    acc_ref[...] = jnp.zeros(acc_ref.shape, F32)

    n_groups = 2
    items = [(c, g) for c in range(n_chunks) for g in range(n_groups)]

    def lanes_of(item):
        return slice(item[1] * tq, (item[1] + 1) * tq)

    def scores(item):
        return jnp.dot(k_ref[item[0] * tk:(item[0] + 1) * tk, :], qp_ref[:, lanes_of(item)],
                       preferred_element_type=F32)

    def running_max(item, s):
        m_prev = m_ref[:, lanes_of(item)]
        m_new = jnp.maximum(m_prev, jnp.max(s, axis=0, keepdims=True))
        m_ref[:, lanes_of(item)] = m_new
        return m_new, jnp.exp2(m_prev - m_new)

    assert n_groups >= 2
    n_items = len(items)
    s_buf = {0: scores(items[0])}
    if n_items > 1:
        s_buf[1] = scores(items[1])
    stats = {0: running_max(items[0], s_buf[0])}
    for idx, item in enumerate(items):
        if idx + 2 < n_items:
            s_buf[idx + 2] = scores(items[idx + 2])
        if idx + 1 < n_items:
            stats[idx + 1] = running_max(items[idx + 1], s_buf[idx + 1])
        m_new, alpha = stats.pop(idx)
        p = jnp.exp2(s_buf.pop(idx) - m_new).astype(BF16)
        lanes = lanes_of(item)
        v_ext = jnp.concatenate([v_ref[:, item[0] * tk:(item[0] + 1) * tk], ones_ref[...]], axis=0)
        acc_ref[:, lanes] = acc_ref[:, lanes] * alpha + jnp.dot(v_ext, p, preferred_element_type=F32)

    lam = lam_ref[0, 0]
    acc = acc_ref[...]
    inv = 1.0 / acc[V_DIM:V_DIM + 1, :]
    o = acc[:V_DIM, :tq] * inv[:, :tq] - lam * (acc[:V_DIM, tq:] * inv[:, tq:])
    ms = jnp.mean(o * o, axis=0, keepdims=True)
    o = o * lax.rsqrt(ms + LN_EPS) * g_ref[...] * out_scale
    o_ref[...] = o.astype(o_ref.dtype)


def diff_attention(q_fm, k_tok, v_fm, lam, norm_g, lam_init, n_batch, tq, tk):
    n_heads = q_fm.shape[0] // V_DIM
    sq = q_fm.shape[1] // n_batch
    nq = sq // tq
    l_keys = k_tok.shape[0] // n_batch
    nc = l_keys // tk
    ones = jnp.zeros((BF16_ROWS, tk), BF16).at[0].set(1.0)
    return pl.pallas_call(
        functools.partial(_attn_kernel, tq=tq, tk=tk, n_chunks=nc, out_scale=1.0 - lam_init),
        grid=(n_batch, n_heads, nq),
        in_specs=[
            pl.BlockSpec(memory_space=pltpu.SMEM),
            pl.BlockSpec((V_DIM, tq), lambda b, h, i: (h, b * nq + i)),
            pl.BlockSpec((l_keys, V_DIM), lambda b, h, i: (b, h)),
            pl.BlockSpec((V_DIM, l_keys), lambda b, h, i: (h, b)),
            pl.BlockSpec((BF16_ROWS, tk), lambda b, h, i: (0, 0)),
            pl.BlockSpec((V_DIM, 1), lambda b, h, i: (0, 0)),
        ],
        out_specs=pl.BlockSpec((V_DIM, tq), lambda b, h, i: (h, b * nq + i)),
        out_shape=jax.ShapeDtypeStruct(q_fm.shape, BF16),
        scratch_shapes=[
            pltpu.VMEM((V_DIM, 2 * tq), BF16),
            pltpu.VMEM((1, 2 * tq), F32),
            pltpu.VMEM((V_ROWS, 2 * tq), F32),
        ],
        compiler_params=_params("parallel", "parallel", "parallel"),
        name="diff_attention",
    )(lam, q_fm, k_tok, v_fm, ones, norm_g)


def _ab_in_kernel(h_ref, w_ref, cos_ref, sin_ref, lng_ref, lnb_ref, wst_ref, bs_ref,
                  q_ref, k_ref, v_ref, g_ref, *, qk_w, sgu_half):
    p = jnp.dot(w_ref[...], h_ref[...], preferred_element_type=F32)
    cs, sn = cos_ref[...], sin_ref[...]
    half = HEAD_DIM // 2
    k_rows = []
    for blk in range(qk_w // HEAD_DIM):
        r = blk * HEAD_DIM
        a, b = p[r:r + half], p[r + half:r + HEAD_DIM]
        q_ref[r:r + half, :] = ((a * cs - b * sn) * Q_SCALE_LOG2).astype(BF16)
        q_ref[r + half:r + HEAD_DIM, :] = ((a * sn + b * cs) * Q_SCALE_LOG2).astype(BF16)
        a, b = p[qk_w + r:qk_w + r + half], p[qk_w + r + half:qk_w + r + HEAD_DIM]
        k_rows += [a * cs - b * sn, a * sn + b * cs]
    k_ref[...] = jnp.concatenate(k_rows, axis=0).T.astype(BF16)
    v_ref[...] = p[2 * qk_w:3 * qk_w].astype(BF16)

    gu = p[3 * qk_w:3 * qk_w + sgu_half]
    v = _gelu_tanh(p[3 * qk_w + sgu_half:])
    mu = jnp.mean(v, axis=0, keepdims=True)
    vc = v - mu
    var = jnp.mean(vc * vc, axis=0, keepdims=True)
    vb = (vc * lax.rsqrt(var + LN_EPS) * lng_ref[...] + lnb_ref[...]).astype(BF16)
    gc = sgu_half // SGU_GROUPS
    for g in range(SGU_GROUPS):
        for j in range(vb.shape[1] // CHUNK):
            rows, cols = slice(g * gc, (g + 1) * gc), slice(j * CHUNK, (j + 1) * CHUNK)
            s = jnp.dot(vb[rows, cols], wst_ref[g], preferred_element_type=F32) + bs_ref[g]
            g_ref[rows, cols] = (_gelu_tanh(gu[rows, cols]) * s).astype(BF16)


def ab_in_proj(h_bf, w_in_t, cos_tab, sin_tab, ln_g, ln_b, ws_t, bs, tn=TOKEN_TILE):
    d, n = h_bf.shape
    tn = min(tn, n)
    qk_w = w_in_t.shape[0] // 4
    sgu_half = (w_in_t.shape[0] - 3 * qk_w) // 2
    n_pos = cos_tab.shape[1] // tn
    tok = lambda rows: pl.BlockSpec((rows, tn), lambda j: (0, j))
    pos = pl.BlockSpec((HEAD_DIM // 2, tn), lambda j: (0, j % n_pos))
    return pl.pallas_call(
        functools.partial(_ab_in_kernel, qk_w=qk_w, sgu_half=sgu_half),
        grid=(n // tn,),
        in_specs=[
            tok(d),
            pl.BlockSpec(w_in_t.shape, lambda j: (0, 0)),
            pos, pos,
            pl.BlockSpec((sgu_half, 1), lambda j: (0, 0)),
            pl.BlockSpec((sgu_half, 1), lambda j: (0, 0)),
            pl.BlockSpec((SGU_GROUPS, CHUNK, CHUNK), lambda j: (0, 0, 0)),
            pl.BlockSpec((SGU_GROUPS, 1, CHUNK), lambda j: (0, 0, 0)),
        ],
        out_specs=[tok(qk_w), pl.BlockSpec((tn, qk_w), lambda j: (j, 0)), tok(qk_w), tok(sgu_half)],
        out_shape=[jax.ShapeDtypeStruct((qk_w, n), BF16), jax.ShapeDtypeStruct((n, qk_w), BF16),
                   jax.ShapeDtypeStruct((qk_w, n), BF16), jax.ShapeDtypeStruct((sgu_half, n), BF16)],
        compiler_params=_params("parallel"),
        name="ab_in_proj",
    )(h_bf, w_in_t, cos_tab, sin_tab, ln_g, ln_b, ws_t, bs)


POOL_TILE = 256
POOL_HALO = 128


def _pool_kernel(left_ref, mid_ref, right_ref, wg_ref, scale_ref, wout_ref, o_ref, *, seq_len):
    j = pl.program_id(0)
    t0 = j * POOL_TILE
    seq_start = (t0 // seq_len) * seq_len
    win = POOL_TILE + 2 * POOL_HALO
    h_all = jnp.concatenate([left_ref[...], mid_ref[...], right_ref[...]], axis=1)
    h_hi = h_all.astype(BF16)
    h_lo = (h_all - h_hi.astype(F32)).astype(BF16)
    tau = t0 - POOL_HALO + lax.broadcasted_iota(jnp.int32, (win, POOL_TILE), 0)
    t = t0 + lax.broadcasted_iota(jnp.int32, (win, POOL_TILE), 1)
    gc = mid_ref.shape[0] // len(POOL_WINDOWS)
    outs = []
    for g, w in enumerate(POOL_WINDOWS):
        lo = jnp.maximum(t - w // 2, seq_start)
        hi = jnp.minimum(t + (w - w // 2), seq_start + seq_len)
        band = jnp.where((tau >= lo) & (tau < hi), 1.0, 0.0).astype(BF16)
        cnt = (hi - lo)[0:1, :].astype(F32)
        sl = slice(g * gc, (g + 1) * gc)
        ssum = (jnp.dot(h_hi[sl], band, preferred_element_type=F32)
                + jnp.dot(h_lo[sl], band, preferred_element_type=F32))
        m = ssum / cnt - mid_ref[sl, :]
        m = jnp.dot(wg_ref[g], m.astype(BF16), preferred_element_type=F32)
        outs.append(m)
    m_all = (jnp.concatenate(outs, axis=0) * scale_ref[...]).astype(BF16)
    o_ref[...] = jnp.dot(wout_ref[...], m_all, preferred_element_type=F32)


def pool_mixer_fm(h, wg_t, scale, wout_t, seq_len):
    c, n = h.shape
    nt = n // POOL_TILE
    r = POOL_TILE // POOL_HALO
    nh = n // POOL_HALO
    return pl.pallas_call(
        functools.partial(_pool_kernel, seq_len=seq_len),
        grid=(nt,),
        in_specs=[
            pl.BlockSpec((c, POOL_HALO), lambda j: (0, jnp.maximum(j * r - 1, 0))),
            pl.BlockSpec((c, POOL_TILE), lambda j: (0, j)),
            pl.BlockSpec((c, POOL_HALO), lambda j: (0, jnp.minimum(j * r + r, nh - 1))),
            pl.BlockSpec(wg_t.shape, lambda j: (0, 0, 0)),
            pl.BlockSpec((c, 1), lambda j: (0, 0)),
            pl.BlockSpec(wout_t.shape, lambda j: (0, 0)),
        ],
        out_specs=pl.BlockSpec((wout_t.shape[0], POOL_TILE), lambda j: (0, j)),
        out_shape=jax.ShapeDtypeStruct((wout_t.shape[0], n), F32),
        compiler_params=_params("parallel"),
        name="pool_mixer",
    )(h, h, h, wg_t, scale, wout_t)


def _sort16_desc(xs):
    xs = list(xs)
    n = len(xs)
    k = 2
    while k <= n:
        j = k // 2
        while j >= 1:
            for i in range(n):
                l = i ^ j
                if l > i:
                    hi, lo = jnp.maximum(xs[i], xs[l]), jnp.minimum(xs[i], xs[l])
                    if (i & k) == 0:
                        xs[i], xs[l] = hi, lo
                    else:
                        xs[i], xs[l] = lo, hi
            j //= 2
        k *= 2
    return xs


def _bitonic_merge_desc(xs):
    xs = list(xs)
    n = len(xs)
    j = n // 2
    while j >= 1:
        for i in range(n):
            l = i ^ j
            if l > i:
                xs[i], xs[l] = jnp.maximum(xs[i], xs[l]), jnp.minimum(xs[i], xs[l])
        j //= 2
    return xs


def _top16_of_two(a, b):
    n = len(a)
    return _bitonic_merge_desc([jnp.maximum(a[i], b[n - 1 - i]) for i in range(n)])


def _top16_desc(vals):
    groups = [_sort16_desc(vals[i:i + P_TOPK]) for i in range(0, len(vals), P_TOPK)]
    while len(groups) > 1:
        groups = [_top16_of_two(groups[i], groups[i + 1]) for i in range(0, len(groups), 2)]
    return groups[0]


def _peer_route_kernel(h_ref, wq_ref, keys_ref, lrow_ref, e1_ref, rank2_ref, e2_ref, s_s):
    q = jnp.dot(wq_ref[...], h_ref[...], preferred_element_type=F32).astype(BF16)
    n_half = h_ref.shape[1] // LANE

    def sheet(p, rows):
        return jnp.concatenate([s_s[half, p, rows, :] for half in range(n_half)], axis=1)

    for hp in range(2 * P_HEADS):
        h, p = divmod(hp, 2)
        s = jnp.dot(keys_ref[hp], q[hp * N_KEYS:(hp + 1) * N_KEYS, :], preferred_element_type=F32)
        for half in range(n_half):
            s_s[half, p, pl.ds(h, N_KEYS, stride=P_HEADS), :] = s[:, half * LANE:(half + 1) * LANE]

    s1 = [sheet(0, slice(k * P_HEADS, (k + 1) * P_HEADS)) for k in range(N_KEYS)]
    s2 = [sheet(1, slice(k * P_HEADS, (k + 1) * P_HEADS)) for k in range(N_KEYS)]
    a = _top16_desc(s1)
    b = _top16_desc(s2)
    cand = [[a[r1] + b[r2] for r2 in range(P_TOPK)] for r1 in range(P_TOPK)]
    rows = [list(r) for r in cand]
    while len(rows) > 1:
        rows = [_top16_of_two(rows[i], rows[i + 1]) for i in range(0, len(rows), 2)]
    best = rows[0]
    thr = best[P_TOPK - 1]
    z = None
    for r in range(P_TOPK):
        e = jnp.exp(best[r] - best[0])
        z = e if z is None else z + e
    inv_z = 1.0 / z
    counts = []
    for r1 in range(P_TOPK):
        cnt = jnp.zeros_like(thr)
        for r2 in range(P_TOPK):
            cnt = cnt + jnp.where(cand[r1][r2] >= thr, 1.0, 0.0)
        counts.append(cnt)

    def per_key(k, carry):
        base = pl.multiple_of(k * P_HEADS, P_HEADS)
        s1k = sheet(0, pl.ds(base, P_HEADS))
        s2k = sheet(1, pl.ds(base, P_HEADS))
        lrow = jnp.zeros_like(thr)
        rank = jnp.zeros_like(thr)
        for r in range(P_TOPK):
            lrow = jnp.where(s1k == a[r], counts[r], lrow)
            rank = jnp.where(b[r] > s2k, float(r + 1), rank)
        lrow_ref[k] = lrow
        e1_ref[k] = jnp.exp(s1k - a[0])
        rank2_ref[k] = rank
        e2_ref[k] = jnp.exp(s2k - b[0]) * inv_z
        return carry

    lax.fori_loop(0, N_KEYS, per_key, 0, unroll=8)


def peer_route(h_bf, wq_t, keys_bf, tn=2 * LANE):
    d, n = h_bf.shape
    rows = N_KEYS * P_HEADS
    spec = pl.BlockSpec((N_KEYS, P_HEADS, tn), lambda j: (0, 0, j))
    sheet = lambda dt: jax.ShapeDtypeStruct((N_KEYS, P_HEADS, n), dt)
    return pl.pallas_call(
        _peer_route_kernel,
        grid=(n // tn,),
        in_specs=[
            pl.BlockSpec((d, tn), lambda j: (0, j)),
            pl.BlockSpec(wq_t.shape, lambda j: (0, 0)),
            pl.BlockSpec(keys_bf.shape, lambda j: (0, 0, 0)),
        ],
        out_specs=[spec] * 4,
        out_shape=[sheet(F32)] * 4,
        scratch_shapes=[pltpu.VMEM((tn // LANE, 2, rows, LANE), F32)],
        compiler_params=_params("parallel"),
        name="peer_route",
    )(h_bf, wq_t, keys_bf)


PEER_TOKEN_TILE = 1024
PEER_I1_TILE = 8


def _peer_dense_kernel(h_ref, u_ref, vt_ref, lrow_ref, e1_ref, rank2_ref, e2_ref, o_ref, w_s, *, i1_tile):
    e = pl.program_id(1)

    @pl.when(e == 0)
    def _():
        o_ref[...] = jnp.zeros(o_ref.shape, F32)

    act = _gelu_tanh(jnp.dot(u_ref[...], h_ref[...], preferred_element_type=F32).astype(BF16))
    tn = act.shape[1]
    zero = jnp.zeros((), BF16)
    for il in range(i1_tile):
        lr = lrow_ref[il * P_HEADS:(il + 1) * P_HEADS, :]
        e1 = e1_ref[il * P_HEADS:(il + 1) * P_HEADS, :]
        gate = None
        for h in range(P_HEADS):
            lb = jnp.broadcast_to(lr[h:h + 1, :], (N_KEYS, tn)).astype(BF16)
            eb = jnp.broadcast_to(e1[h:h + 1, :], (N_KEYS, tn)).astype(BF16)
            keys = slice(h * N_KEYS, (h + 1) * N_KEYS)
            term = jnp.where(rank2_ref[keys, :] < lb, e2_ref[keys, :], zero) * eb
            gate = term if gate is None else gate + term
        w_s[il * N_KEYS:(il + 1) * N_KEYS, :] = gate * act[il * N_KEYS:(il + 1) * N_KEYS, :]
    o_ref[...] += jnp.dot(vt_ref[...], w_s[...], preferred_element_type=F32)


def peer_dense(h_bf, u_bf, vt_bf, layer, lrow, e1, rank2, e2, tn=PEER_TOKEN_TILE, i1_tile=PEER_I1_TILE):
    d, n = h_bf.shape
    n_exp = u_bf.shape[1]
    tn = min(tn, n)
    et = i1_tile * N_KEYS
    table = pl.BlockSpec((P_HEADS * N_KEYS, tn), lambda j, e: (0, j))
    return pl.pallas_call(
        functools.partial(_peer_dense_kernel, i1_tile=i1_tile),
        grid=(n // tn, n_exp // et),
        in_specs=[
            pl.BlockSpec((d, tn), lambda j, e: (0, j)),
            pl.BlockSpec((None, et, d), lambda j, e: (layer, e, 0)),
            pl.BlockSpec((None, d, et), lambda j, e: (layer, 0, e)),
            pl.BlockSpec((i1_tile * P_HEADS, tn), lambda j, e: (e, j)),
            pl.BlockSpec((i1_tile * P_HEADS, tn), lambda j, e: (e, j)),
            table, table,
        ],
        out_specs=pl.BlockSpec((d, tn), lambda j, e: (0, j)),
        out_shape=jax.ShapeDtypeStruct((d, n), F32),
        scratch_shapes=[pltpu.VMEM((et, tn), BF16)],
        compiler_params=_params("parallel", "arbitrary"),
        name="peer_dense",
    )(h_bf, u_bf, vt_bf, lrow, e1, rank2, e2)


def peer_fm(h_bf, wq_t, keys_bf, u_bf, vt_bf, layer):
    lrow, e1, rank2, e2 = peer_route(h_bf, wq_t, keys_bf)
    rows, n = N_KEYS * P_HEADS, h_bf.shape[1]
    head_major = lambda t: jnp.transpose(t, (1, 0, 2)).reshape(rows, n).astype(BF16)
    return peer_dense(h_bf, u_bf, vt_bf, layer, lrow.reshape(rows, n), e1.reshape(rows, n),
                      head_major(rank2), head_major(e2))


def _rope_tables_fm(rows):
    row = jnp.repeat(jnp.arange(rows, dtype=F32), GRID_W)
    col = jnp.tile(jnp.arange(GRID_W, dtype=F32), rows)
    inv = ROPE_BASE ** (-jnp.arange(ROPE_AXIS_PAIRS, dtype=F32) / ROPE_AXIS_PAIRS)
    ang = jnp.concatenate([inv[:, None] * row[None, :], inv[:, None] * col[None, :]], axis=0)
    return jnp.cos(ang), jnp.sin(ang)


def _kv_all(kc_tok, kx_tok, vc_fm, vx_fm, n_batch):
    w = kx_tok.shape[1]
    k = jnp.concatenate([kc_tok.reshape(n_batch, -1, w), kx_tok.reshape(n_batch, -1, w)], axis=1)
    v = jnp.concatenate([vc_fm.reshape(w, n_batch, -1), vx_fm.reshape(w, n_batch, -1)], axis=2)
    return k.reshape(-1, w), v.reshape(w, -1)


def _pick_tk(l):
    for tk in (768, 512, 256, 128):
        if l % tk == 0:
            return tk
    raise ValueError(f"unsupported key length {l}")


def kernel(x, c, ctx, c_ctx, ada_w, ada_b, ln_g, ln_b, ab_w_in, ab_w_out, diff_lam, diff_norm_g,
           sgu_ln_g, sgu_ln_b, sgu_w, sgu_b, pool_w_in, pool_w_grp, pool_scale, pool_w_out,
           peer_wq, peer_keys, peer_u, peer_v):
    bsz, s_len, d = x.shape
    c_len = ctx.shape[1]
    depth = ada_w.shape[0]
    qk_w = ab_w_in.shape[2] // 4
    alpha = (2.0 * depth) ** 0.25
    last_ctx_read = 2 * ((depth - 1) // 2)
    cos, sin = _rope_tables_fm(s_len // GRID_W)
    c_tile = min(TOKEN_TILE, bsz * c_len)
    cos_c = jnp.ones((HEAD_DIM // 2, c_tile), F32)
    sin_c = jnp.zeros((HEAD_DIM // 2, c_tile), F32)

    xs = x.reshape(bsz * s_len, d).T
    cs = ctx.reshape(bsz * c_len, d).T

    cond = jnp.zeros((d, LANE), F32)
    cond = cond.at[:, :bsz].set(jax.nn.silu(c).T).at[:, bsz].set(jax.nn.silu(c_ctx))
    cond_t = cond.T.astype(BF16)
    u_bf = peer_u.astype(BF16)
    vt_bf = jnp.transpose(peer_v, (0, 2, 1)).astype(BF16)
    mods_x, mods_c = [], []
    for i in range(depth):
        ada = matmul_fm([(cond_t, ada_w[i])], F32).T + ada_b[i][:, None]
        mods_x.append([ada[k * d:(k + 1) * d, :bsz] for k in range(6)])
        mods_c.append([ada[k * d:(k + 1) * d, bsz:bsz + 1] for k in range(6)])

    def uses_ctx(i):
        return i < depth and (i < last_ctx_read or i % 2 == 0)

    _, hx = ln_mod_fm(xs, mod=(mods_x[0][0], mods_x[0][1]))
    hc = ln_mod_fm(cs, mod=(mods_c[0][0], mods_c[0][1]))[1] if uses_ctx(0) else None

    for i in range(depth):
        j = i // 2
        even = i % 2 == 0
        ctx_out = i < last_ctx_read
        last = i == depth - 1
        sh1, sc1, g1, sh2, sc2, g2 = mods_x[i]
        csh1, csc1, cg1, csh2, csc2, cg2 = mods_c[i]

        if even:
            lam_init = 0.8 - 0.6 * math.exp(-0.3 * i)
            lq1, lk1, lq2, lk2 = diff_lam[j].astype(F32)
            lam = (jnp.exp(jnp.sum(lq1 * lk1)) - jnp.exp(jnp.sum(lq2 * lk2)) + lam_init).reshape(1, 1)
            w_in_t = ab_w_in[j].T.astype(BF16)
            w_out_t = ab_w_out[j].T.astype(BF16)
            norm_g = diff_norm_g[j][:, None]
            sgu_args = (sgu_ln_g[j][:, None], sgu_ln_b[j][:, None],
                        jnp.transpose(sgu_w[j], (0, 2, 1)).astype(BF16), sgu_b[j][:, None, :])
            q_x, k_x, v_x, g_x = ab_in_proj(hx, w_in_t, cos, sin, *sgu_args)
            q_c, k_c, v_c, g_c = ab_in_proj(hc, w_in_t, cos_c, sin_c, *sgu_args)
            k_all, v_all = _kv_all(k_c, k_x, v_c, v_x, bsz)
            a_x = diff_attention(q_x, k_all, v_all, lam, norm_g, lam_init, bsz,
                                 tq=min(ATTN_Q_TILE, s_len), tk=_pick_tk(c_len + s_len))
            yx = matmul_fm([(w_out_t[:, :qk_w], a_x), (w_out_t[:, qk_w:], g_x)], F32)
            if ctx_out:
                a_c = diff_attention(q_c, k_c, v_c, lam, norm_g, lam_init, bsz, tq=c_len, tk=_pick_tk(c_len))
                yc = matmul_fm([(w_out_t[:, :qk_w], a_c), (w_out_t[:, qk_w:], g_c)], F32)
        else:
            p_in_t = pool_w_in[j].T.astype(BF16)
            wg_t = jnp.transpose(pool_w_grp[j], (0, 2, 1)).astype(BF16)
            p_out_t = pool_w_out[j].T.astype(BF16)
            scale = pool_scale[j][:, None]
            yx = pool_mixer_fm(matmul_fm([(p_in_t, hx)], F32), wg_t, scale, p_out_t, s_len)
            if ctx_out:
                yc = pool_mixer_fm(matmul_fm([(p_in_t, hc)], F32), wg_t, scale, p_out_t, c_len)

        wq_t = peer_wq[i].T.astype(BF16)
        keys_bf = peer_keys[i].reshape(2 * P_HEADS, N_KEYS, -1).astype(BF16)
        next_mod_c = (mods_c[i + 1][0], mods_c[i + 1][1]) if uses_ctx(i + 1) else None
        if ctx_out:
            cs, hc = ln_mod_fm(cs, resid=(yc, cg1, ln_g[i, 0], ln_b[i, 0]), mod=(csh2, csc2), alpha=alpha)
            fc = peer_fm(hc, wq_t, keys_bf, u_bf, vt_bf, i)
            if uses_ctx(i + 1):
                cs, hc = ln_mod_fm(cs, resid=(fc, cg2, ln_g[i, 1], ln_b[i, 1]), mod=next_mod_c, alpha=alpha)
        elif next_mod_c is not None:
            hc = ln_mod_fm(cs, mod=next_mod_c)[1]

        xs, hx = ln_mod_fm(xs, resid=(yx, g1, ln_g[i, 0], ln_b[i, 0]), mod=(sh2, sc2), alpha=alpha)
        fx = peer_fm(hx, wq_t, keys_bf, u_bf, vt_bf, i)
        next_mod_x = None if last else (mods_x[i + 1][0], mods_x[i + 1][1])
        xs, hx = ln_mod_fm(xs, resid=(fx, g2, ln_g[i, 1], ln_b[i, 1]), mod=next_mod_x, alpha=alpha)

    return xs.T.reshape(bsz, s_len, d)
```

```python
import functools
import math

import jax
import jax.numpy as jnp
from jax import lax
from jax.experimental import pallas as pl
from jax.experimental.pallas import tpu as pltpu

F32 = jnp.float32
BF16 = jnp.bfloat16
U32 = jnp.uint32

GRID_W = 64
LN_EPS = 1e-5
HEAD_DIM = 64
V_DIM = 2 * HEAD_DIM
ATTN_SCALE = HEAD_DIM ** -0.5
Q_SCALE_LOG2 = ATTN_SCALE * math.log2(math.e)
ROPE_BASE = 10000.0
ROPE_AXIS_PAIRS = HEAD_DIM // 4
CHUNK = 128
SGU_GROUPS = 4
POOL_WINDOWS = (2, 4, 8, 16)
P_HEADS = 8
N_KEYS = 128
P_TOPK = 16

LANE = 128
SUBLANE = 8
BF16_ROWS = 16
VMEM_LIMIT = 56 * 1024 * 1024
TOKEN_TILE = 1024


def _params(*sem):
    return pltpu.CompilerParams(dimension_semantics=sem, vmem_limit_bytes=VMEM_LIMIT)


def _gelu_tanh(x):
    c1 = math.sqrt(2.0 / math.pi)
    c2 = c1 * 0.044715
    inner = x * (c1 + c2 * (x * x))
    return x * (0.5 + 0.5 * jnp.tanh(inner))


def _mm_kernel(*refs, n_pairs, has_bias):
    o_ref = refs[-1]
    acc = None
    for i in range(n_pairs):
        part = jnp.dot(refs[2 * i][...].astype(BF16), refs[2 * i + 1][...].astype(BF16),
                       preferred_element_type=F32)
        acc = part if acc is None else acc + part
    if has_bias:
        acc = acc + refs[2 * n_pairs][...]
    o_ref[...] = acc.astype(o_ref.dtype)


def matmul_fm(pairs, out_dtype, bias=None, tn=TOKEN_TILE, tm=None):
    m = pairs[0][0].shape[0]
    n = pairs[0][1].shape[1]
    tn = min(tn, n)
    tm = m if tm is None else tm
    args, specs = [], []
    for w, x in pairs:
        k = w.shape[1]
        args += [w, x]
        specs += [pl.BlockSpec((tm, k), lambda j, i: (i, 0)),
                  pl.BlockSpec((k, tn), lambda j, i: (0, j))]
    if bias is not None:
        args.append(bias)
        specs.append(pl.BlockSpec((tm, 1), lambda j, i: (i, 0)))
    return pl.pallas_call(
        functools.partial(_mm_kernel, n_pairs=len(pairs), has_bias=bias is not None),
        grid=(n // tn, m // tm),
        in_specs=specs,
        out_specs=pl.BlockSpec((tm, tn), lambda j, i: (i, j)),
        out_shape=jax.ShapeDtypeStruct((m, n), out_dtype),
        compiler_params=_params("parallel", "parallel"),
        name="matmul_fm",
    )(*args)


def _ln_mod_kernel(*refs, alpha, has_resid, emit_x, emit_h):
    it = iter(refs)
    x = next(it)[...]
    if has_resid:
        y_ref, gate_ref, lng_ref, lnb_ref = next(it), next(it), next(it), next(it)
    if emit_h:
        shift_ref, scale_ref = next(it), next(it)
    if has_resid:
        z = alpha * x + gate_ref[...] * y_ref[...]
        mu = jnp.mean(z, axis=0, keepdims=True)
        zc = z - mu
        var = jnp.mean(zc * zc, axis=0, keepdims=True)
        x = zc * lax.rsqrt(var + LN_EPS) * lng_ref[...] + lnb_ref[...]
        if emit_x:
            next(it)[...] = x
    if emit_h:
        next(it)[...] = (x * (1.0 + scale_ref[...]) + shift_ref[...]).astype(BF16)


def ln_mod_fm(xs, *, resid=None, mod=None, alpha=1.0, emit_x=True, tn=TOKEN_TILE):
    d, n = xs.shape
    tn = min(tn, n)
    n_seg = (resid[1] if resid is not None else mod[0]).shape[1]
    tps = (n // n_seg) // tn
    tok = pl.BlockSpec((d, tn), lambda j: (0, j))
    seg = pl.BlockSpec((None, d, 1), lambda j: (j // tps, 0, 0))
    col = pl.BlockSpec((d, 1), lambda j: (0, 0))
    args, specs, out_shape, out_specs = [xs], [tok], [], []
    if resid is not None:
        y, gate, ln_g, ln_b = resid
        args += [y, gate.T[:, :, None], ln_g[:, None], ln_b[:, None]]
        specs += [tok, seg, col, col]
        if emit_x:
            out_shape.append(jax.ShapeDtypeStruct((d, n), F32))
            out_specs.append(tok)
    if mod is not None:
        args += [mod[0].T[:, :, None], mod[1].T[:, :, None]]
        specs += [seg, seg]
        out_shape.append(jax.ShapeDtypeStruct((d, n), BF16))
        out_specs.append(tok)
    outs = pl.pallas_call(
        functools.partial(_ln_mod_kernel, alpha=alpha, has_resid=resid is not None,
                          emit_x=emit_x and resid is not None, emit_h=mod is not None),
        grid=(n // tn,),
        in_specs=specs,
        out_specs=out_specs,
        out_shape=out_shape,
        compiler_params=_params("parallel"),
        name="ln_mod",
    )(*args)
    outs = list(outs)
    x_new = outs.pop(0) if (resid is not None and emit_x) else None
    h = outs.pop(0) if mod is not None else None
    return x_new, h


ATTN_Q_TILE = 1024
V_ROWS = V_DIM + BF16_ROWS


def _attn_kernel(lam_ref, q_ref, k_ref, v_ref, ones_ref, g_ref, o_ref, qp_ref, m_ref, acc_ref,
                 *, tq, tk, n_chunks, out_scale):
    q = q_ref[...]
    row = lax.broadcasted_iota(jnp.int32, q.shape, 0)
    zero = jnp.zeros_like(q)
    qp_ref[:, :tq] = jnp.where(row < HEAD_DIM, q, zero)
    qp_ref[:, tq:] = jnp.where(row >= HEAD_DIM, q, zero)
    m_ref[...] = jnp.full(m_ref.shape, -jnp.inf, F32)
    acc_ref[...] = jnp.zeros(acc_ref.shape, F32)

    n_groups = 2
    items = [(c, g) for c in range(n_chunks) for g in range(n_groups)]

    def lanes_of(item):
        return slice(item[1] * tq, (item[1] + 1) * tq)

    def scores(item):
        return jnp.dot(k_ref[item[0] * tk:(item[0] + 1) * tk, :], qp_ref[:, lanes_of(item)],
                       preferred_element_type=F32)

    def running_max(item, s):
        m_prev = m_ref[:, lanes_of(item)]
        m_new = jnp.maximum(m_prev, jnp.max(s, axis=0, keepdims=True))
        m_ref[:, lanes_of(item)] = m_new
        return m_new, jnp.exp2(m_prev - m_new)

    assert n_groups >= 2
    n_items = len(items)
    s_buf = {0: scores(items[0])}
    if n_items > 1:
        s_buf[1] = scores(items[1])
    stats = {0: running_max(items[0], s_buf[0])}
    for idx, item in enumerate(items):
        if idx + 2 < n_items:
            s_buf[idx + 2] = scores(items[idx + 2])
        if idx + 1 < n_items:
            stats[idx + 1] = running_max(items[idx + 1], s_buf[idx + 1])
        m_new, alpha = stats.pop(idx)
        p = jnp.exp2(s_buf.pop(idx) - m_new).astype(BF16)
        lanes = lanes_of(item)
        v_ext = jnp.concatenate([v_ref[:, item[0] * tk:(item[0] + 1) * tk], ones_ref[...]], axis=0)
        acc_ref[:, lanes] = acc_ref[:, lanes] * alpha + jnp.dot(v_ext, p, preferred_element_type=F32)

    lam = lam_ref[0, 0]
    acc = acc_ref[...]
    inv = 1.0 / acc[V_DIM:V_DIM + 1, :]
    o = acc[:V_DIM, :tq] * inv[:, :tq] - lam * (acc[:V_DIM, tq:] * inv[:, tq:])
    ms = jnp.mean(o * o, axis=0, keepdims=True)
    o = o * lax.rsqrt(ms + LN_EPS) * g_ref[...] * out_scale
    o_ref[...] = o.astype(o_ref.dtype)


def diff_attention(q_fm, k_tok, v_fm, lam, norm_g, lam_init, n_batch, tq, tk):
    n_heads = q_fm.shape[0] // V_DIM
    sq = q_fm.shape[1] // n_batch
    nq = sq // tq
    l_keys = k_tok.shape[0] // n_batch
    nc = l_keys // tk
    ones = jnp.zeros((BF16_ROWS, tk), BF16).at[0].set(1.0)
    return pl.pallas_call(
        functools.partial(_attn_kernel, tq=tq, tk=tk, n_chunks=nc, out_scale=1.0 - lam_init),
        grid=(n_batch, n_heads, nq),
        in_specs=[
            pl.BlockSpec(memory_space=pltpu.SMEM),
            pl.BlockSpec((V_DIM, tq), lambda b, h, i: (h, b * nq + i)),
            pl.BlockSpec((l_keys, V_DIM), lambda b, h, i: (b, h)),
            pl.BlockSpec((V_DIM, l_keys), lambda b, h, i: (h, b)),
            pl.BlockSpec((BF16_ROWS, tk), lambda b, h, i: (0, 0)),
            pl.BlockSpec((V_DIM, 1), lambda b, h, i: (0, 0)),
        ],
        out_specs=pl.BlockSpec((V_DIM, tq), lambda b, h, i: (h, b * nq + i)),
        out_shape=jax.ShapeDtypeStruct(q_fm.shape, BF16),
        scratch_shapes=[
            pltpu.VMEM((V_DIM, 2 * tq), BF16),
            pltpu.VMEM((1, 2 * tq), F32),
            pltpu.VMEM((V_ROWS, 2 * tq), F32),
        ],
        compiler_params=_params("parallel", "parallel", "parallel"),
        name="diff_attention",
    )(lam, q_fm, k_tok, v_fm, ones, norm_g)


def _ab_in_kernel(h_ref, w_ref, cos_ref, sin_ref, lng_ref, lnb_ref, wst_ref, bs_ref,
                  q_ref, k_ref, v_ref, g_ref, *, qk_w, sgu_half):
    p = jnp.dot(w_ref[...], h_ref[...], preferred_element_type=F32)
    cs, sn = cos_ref[...], sin_ref[...]
    half = HEAD_DIM // 2
    k_rows = []
    for blk in range(qk_w // HEAD_DIM):
        r = blk * HEAD_DIM
        a, b = p[r:r + half], p[r + half:r + HEAD_DIM]
        q_ref[r:r + half, :] = ((a * cs - b * sn) * Q_SCALE_LOG2).astype(BF16)
        q_ref[r + half:r + HEAD_DIM, :] = ((a * sn + b * cs) * Q_SCALE_LOG2).astype(BF16)
        a, b = p[qk_w + r:qk_w + r + half], p[qk_w + r + half:qk_w + r + HEAD_DIM]
        k_rows += [a * cs - b * sn, a * sn + b * cs]
    k_ref[...] = jnp.concatenate(k_rows, axis=0).T.astype(BF16)
    v_ref[...] = p[2 * qk_w:3 * qk_w].astype(BF16)

    gu = p[3 * qk_w:3 * qk_w + sgu_half]
    v = _gelu_tanh(p[3 * qk_w + sgu_half:])
    mu = jnp.mean(v, axis=0, keepdims=True)
    vc = v - mu
    var = jnp.mean(vc * vc, axis=0, keepdims=True)
    vb = (vc * lax.rsqrt(var + LN_EPS) * lng_ref[...] + lnb_ref[...]).astype(BF16)
    gc = sgu_half // SGU_GROUPS
    for g in range(SGU_GROUPS):
        for j in range(vb.shape[1] // CHUNK):
            rows, cols = slice(g * gc, (g + 1) * gc), slice(j * CHUNK, (j + 1) * CHUNK)
            s = jnp.dot(vb[rows, cols], wst_ref[g], preferred_element_type=F32) + bs_ref[g]
            g_ref[rows, cols] = (_gelu_tanh(gu[rows, cols]) * s).astype(BF16)


def ab_in_proj(h_bf, w_in_t, cos_tab, sin_tab, ln_g, ln_b, ws_t, bs, tn=TOKEN_TILE):
    d, n = h_bf.shape
    tn = min(tn, n)
    qk_w = w_in_t.shape[0] // 4
    sgu_half = (w_in_t.shape[0] - 3 * qk_w) // 2
    n_pos = cos_tab.shape[1] // tn
    tok = lambda rows: pl.BlockSpec((rows, tn), lambda j: (0, j))
    pos = pl.BlockSpec((HEAD_DIM // 2, tn), lambda j: (0, j % n_pos))
    return pl.pallas_call(
        functools.partial(_ab_in_kernel, qk_w=qk_w, sgu_half=sgu_half),
        grid=(n // tn,),
        in_specs=[
            tok(d),
            pl.BlockSpec(w_in_t.shape, lambda j: (0, 0)),
            pos, pos,
            pl.BlockSpec((sgu_half, 1), lambda j: (0, 0)),
            pl.BlockSpec((sgu_half, 1), lambda j: (0, 0)),
            pl.BlockSpec((SGU_GROUPS, CHUNK, CHUNK), lambda j: (0, 0, 0)),
            pl.BlockSpec((SGU_GROUPS, 1, CHUNK), lambda j: (0, 0, 0)),
        ],
        out_specs=[tok(qk_w), pl.BlockSpec((tn, qk_w), lambda j: (j, 0)), tok(qk_w), tok(sgu_half)],
        out_shape=[jax.ShapeDtypeStruct((qk_w, n), BF16), jax.ShapeDtypeStruct((n, qk_w), BF16),
                   jax.ShapeDtypeStruct((qk_w, n), BF16), jax.ShapeDtypeStruct((sgu_half, n), BF16)],
        compiler_params=_params("parallel"),
        name="ab_in_proj",
    )(h_bf, w_in_t, cos_tab, sin_tab, ln_g, ln_b, ws_t, bs)


POOL_TILE = 512
POOL_HALO = 128


def _pool_kernel(left_ref, mid_ref, right_ref, wg_ref, scale_ref, wout_ref, o_ref, *, seq_len, tile):
    j = pl.program_id(0)
    t0 = j * tile
    seq_start = (t0 // seq_len) * seq_len
    win = tile + 2 * POOL_HALO
    h_all = jnp.concatenate([left_ref[...], mid_ref[...], right_ref[...]], axis=1)
    h_hi = h_all.astype(BF16)
    h_lo = (h_all - h_hi.astype(F32)).astype(BF16)
    tau = t0 - POOL_HALO + lax.broadcasted_iota(jnp.int32, (win, tile), 0)
    t = t0 + lax.broadcasted_iota(jnp.int32, (win, tile), 1)
    gc = mid_ref.shape[0] // len(POOL_WINDOWS)
    outs = []
    for g, w in enumerate(POOL_WINDOWS):
        lo = jnp.maximum(t - w // 2, seq_start)
        hi = jnp.minimum(t + (w - w // 2), seq_start + seq_len)
        band = jnp.where((tau >= lo) & (tau < hi), 1.0, 0.0).astype(BF16)
        cnt = (hi - lo)[0:1, :].astype(F32)
        sl = slice(g * gc, (g + 1) * gc)
        ssum = (jnp.dot(h_hi[sl], band, preferred_element_type=F32)
                + jnp.dot(h_lo[sl], band, preferred_element_type=F32))
        m = ssum / cnt - mid_ref[sl, :]
        m = jnp.dot(wg_ref[g], m.astype(BF16), preferred_element_type=F32)
        outs.append(m)
    m_all = (jnp.concatenate(outs, axis=0) * scale_ref[...]).astype(BF16)
    o_ref[...] = jnp.dot(wout_ref[...], m_all, preferred_element_type=F32)


def pool_mixer_fm(h, wg_t, scale, wout_t, seq_len):
    c, n = h.shape
    tile = min(POOL_TILE, seq_len)
    nt = n // tile
    r = tile // POOL_HALO
    nh = n // POOL_HALO
    return pl.pallas_call(
        functools.partial(_pool_kernel, seq_len=seq_len, tile=tile),
        grid=(nt,),
        in_specs=[
            pl.BlockSpec((c, POOL_HALO), lambda j: (0, jnp.maximum(j * r - 1, 0))),
            pl.BlockSpec((c, tile), lambda j: (0, j)),
            pl.BlockSpec((c, POOL_HALO), lambda j: (0, jnp.minimum(j * r + r, nh - 1))),
            pl.BlockSpec(wg_t.shape, lambda j: (0, 0, 0)),
            pl.BlockSpec((c, 1), lambda j: (0, 0)),
            pl.BlockSpec(wout_t.shape, lambda j: (0, 0)),
        ],
        out_specs=pl.BlockSpec((wout_t.shape[0], tile), lambda j: (0, j)),
        out_shape=jax.ShapeDtypeStruct((wout_t.shape[0], n), F32),
        compiler_params=_params("parallel"),
        name="pool_mixer",
    )(h, h, h, wg_t, scale, wout_t)


def _sort16_desc(xs):
    xs = list(xs)
    n = len(xs)
    k = 2
    while k <= n:
        j = k // 2
        while j >= 1:
            for i in range(n):
                l = i ^ j
                if l > i:
                    hi, lo = jnp.maximum(xs[i], xs[l]), jnp.minimum(xs[i], xs[l])
                    if (i & k) == 0:
                        xs[i], xs[l] = hi, lo
                    else:
                        xs[i], xs[l] = lo, hi
            j //= 2
        k *= 2
    return xs


def _bitonic_merge_desc(xs):
    xs = list(xs)
    n = len(xs)
    j = n // 2
    while j >= 1:
        for i in range(n):
            l = i ^ j
            if l > i:
                xs[i], xs[l] = jnp.maximum(xs[i], xs[l]), jnp.minimum(xs[i], xs[l])
        j //= 2
    return xs


def _top16_of_two(a, b):
    n = len(a)
    return _bitonic_merge_desc([jnp.maximum(a[i], b[n - 1 - i]) for i in range(n)])


def _top16_desc(vals):
    groups = [_sort16_desc(vals[i:i + P_TOPK]) for i in range(0, len(vals), P_TOPK)]
    while len(groups) > 1:
        groups = [_top16_of_two(groups[i], groups[i + 1]) for i in range(0, len(groups), 2)]
    return groups[0]


def _peer_route_kernel(h_ref, wq_ref, keys_ref, lrow_ref, e1_ref, rank2_ref, e2_ref, s_s):
    q = jnp.dot(wq_ref[...], h_ref[...], preferred_element_type=F32).astype(BF16)
    n_half = h_ref.shape[1] // LANE

    def sheet(p, rows):
        return jnp.concatenate([s_s[half, p, rows, :] for half in range(n_half)], axis=1)

    for hp in range(2 * P_HEADS):
        h, p = divmod(hp, 2)
        s = jnp.dot(keys_ref[hp], q[hp * N_KEYS:(hp + 1) * N_KEYS, :], preferred_element_type=F32)
        for half in range(n_half):
            s_s[half, p, pl.ds(h, N_KEYS, stride=P_HEADS), :] = s[:, half * LANE:(half + 1) * LANE]

    s1 = [sheet(0, slice(k * P_HEADS, (k + 1) * P_HEADS)) for k in range(N_KEYS)]
    s2 = [sheet(1, slice(k * P_HEADS, (k + 1) * P_HEADS)) for k in range(N_KEYS)]
    a = _top16_desc(s1)
    b = _top16_desc(s2)
    cand = [[a[r1] + b[r2] for r2 in range(P_TOPK)] for r1 in range(P_TOPK)]
    rows = [list(r) for r in cand]
    while len(rows) > 1:
        rows = [_top16_of_two(rows[i], rows[i + 1]) for i in range(0, len(rows), 2)]
    best = rows[0]
    thr = best[P_TOPK - 1]
    z = None
    for r in range(P_TOPK):
        e = jnp.exp(best[r] - best[0])
        z = e if z is None else z + e
    inv_z = 1.0 / z
    counts = []
    for r1 in range(P_TOPK):
        cnt = jnp.zeros_like(thr)
        for r2 in range(P_TOPK):
            cnt = cnt + jnp.where(cand[r1][r2] >= thr, 1.0, 0.0)
        counts.append(cnt)

    def per_key(k, carry):
        base = pl.multiple_of(k * P_HEADS, P_HEADS)
        s1k = sheet(0, pl.ds(base, P_HEADS))
        s2k = sheet(1, pl.ds(base, P_HEADS))
        lrow = jnp.zeros_like(thr)
        rank = jnp.zeros_like(thr)
        for r in range(P_TOPK):
            lrow = jnp.where(s1k == a[r], counts[r], lrow)
            rank = jnp.where(b[r] > s2k, float(r + 1), rank)
        lrow_ref[k] = lrow
        e1_ref[k] = jnp.exp(s1k - a[0])
        rank2_ref[k] = rank
        e2_ref[k] = jnp.exp(s2k - b[0]) * inv_z
        return carry

    lax.fori_loop(0, N_KEYS, per_key, 0, unroll=8)


def peer_route(h_bf, wq_t, keys_bf, tn=2 * LANE):
    d, n = h_bf.shape
    rows = N_KEYS * P_HEADS
    spec = pl.BlockSpec((N_KEYS, P_HEADS, tn), lambda j: (0, 0, j))
    sheet = lambda dt: jax.ShapeDtypeStruct((N_KEYS, P_HEADS, n), dt)
    return pl.pallas_call(
        _peer_route_kernel,
        grid=(n // tn,),
        in_specs=[
            pl.BlockSpec((d, tn), lambda j: (0, j)),
            pl.BlockSpec(wq_t.shape, lambda j: (0, 0)),
            pl.BlockSpec(keys_bf.shape, lambda j: (0, 0, 0)),
        ],
        out_specs=[spec] * 4,
        out_shape=[sheet(F32)] * 4,
        scratch_shapes=[pltpu.VMEM((tn // LANE, 2, rows, LANE), F32)],
        compiler_params=_params("parallel"),
        name="peer_route",
    )(h_bf, wq_t, keys_bf)


PEER_TOKEN_TILE = 1024
PEER_I1_TILE = 8


def _peer_dense_kernel(h_ref, u_ref, vt_ref, lrow_ref, e1_ref, rank2_ref, e2_ref, o_ref, w_s, *, i1_tile):
    e = pl.program_id(1)

    @pl.when(e == 0)
    def _():
        o_ref[...] = jnp.zeros(o_ref.shape, F32)

    act = _gelu_tanh(jnp.dot(u_ref[...], h_ref[...], preferred_element_type=F32).astype(BF16))
    tn = act.shape[1]
    zero = jnp.zeros((), BF16)
    for il in range(i1_tile):
        lr = lrow_ref[il * P_HEADS:(il + 1) * P_HEADS, :]
        e1 = e1_ref[il * P_HEADS:(il + 1) * P_HEADS, :]
        gate = None
        for h in range(P_HEADS):
            lb = jnp.broadcast_to(lr[h:h + 1, :], (N_KEYS, tn)).astype(BF16)
            eb = jnp.broadcast_to(e1[h:h + 1, :], (N_KEYS, tn)).astype(BF16)
            keys = slice(h * N_KEYS, (h + 1) * N_KEYS)
            term = jnp.where(rank2_ref[keys, :] < lb, e2_ref[keys, :], zero) * eb
            gate = term if gate is None else gate + term
        w_s[il * N_KEYS:(il + 1) * N_KEYS, :] = gate * act[il * N_KEYS:(il + 1) * N_KEYS, :]
    o_ref[...] += jnp.dot(vt_ref[...], w_s[...], preferred_element_type=F32)


def peer_dense(h_bf, u_bf, vt_bf, layer, lrow, e1, rank2, e2, tn=PEER_TOKEN_TILE, i1_tile=PEER_I1_TILE):
    d, n = h_bf.shape
    n_exp = u_bf.shape[1]
    tn = min(tn, n)
    et = i1_tile * N_KEYS
    table = pl.BlockSpec((P_HEADS * N_KEYS, tn), lambda j, e: (0, j))
    return pl.pallas_call(
        functools.partial(_peer_dense_kernel, i1_tile=i1_tile),
        grid=(n // tn, n_exp // et),
        in_specs=[
            pl.BlockSpec((d, tn), lambda j, e: (0, j)),
            pl.BlockSpec((None, et, d), lambda j, e: (layer, e, 0)),
            pl.BlockSpec((None, d, et), lambda j, e: (layer, 0, e)),
            pl.BlockSpec((i1_tile * P_HEADS, tn), lambda j, e: (e, j)),
            pl.BlockSpec((i1_tile * P_HEADS, tn), lambda j, e: (e, j)),
            table, table,
        ],
        out_specs=pl.BlockSpec((d, tn), lambda j, e: (0, j)),
        out_shape=jax.ShapeDtypeStruct((d, n), F32),
        scratch_shapes=[pltpu.VMEM((et, tn), BF16)],
        compiler_params=_params("parallel", "arbitrary"),
        name="peer_dense",
    )(h_bf, u_bf, vt_bf, lrow, e1, rank2, e2)


def peer_fm(h_bf, wq_t, keys_bf, u_bf, vt_bf, layer):
    lrow, e1, rank2, e2 = peer_route(h_bf, wq_t, keys_bf)
    rows, n = N_KEYS * P_HEADS, h_bf.shape[1]
    head_major = lambda t: jnp.transpose(t, (1, 0, 2)).reshape(rows, n).astype(BF16)
    return peer_dense(h_bf, u_bf, vt_bf, layer, lrow.reshape(rows, n), e1.reshape(rows, n),
                      head_major(rank2), head_major(e2))


def _rope_tables_fm(rows):
    row = jnp.repeat(jnp.arange(rows, dtype=F32), GRID_W)
    col = jnp.tile(jnp.arange(GRID_W, dtype=F32), rows)
    inv = ROPE_BASE ** (-jnp.arange(ROPE_AXIS_PAIRS, dtype=F32) / ROPE_AXIS_PAIRS)
    ang = jnp.concatenate([inv[:, None] * row[None, :], inv[:, None] * col[None, :]], axis=0)
    return jnp.cos(ang), jnp.sin(ang)


def _kv_all(kc_tok, kx_tok, vc_fm, vx_fm, n_batch):
    w = kx_tok.shape[1]
    k = jnp.concatenate([kc_tok.reshape(n_batch, -1, w), kx_tok.reshape(n_batch, -1, w)], axis=1)
    v = jnp.concatenate([vc_fm.reshape(w, n_batch, -1), vx_fm.reshape(w, n_batch, -1)], axis=2)
    return k.reshape(-1, w), v.reshape(w, -1)


def _pick_tk(l):
    for tk in (768, 512, 256, 128):
        if l % tk == 0:
            return tk
    raise ValueError(f"unsupported key length {l}")


def kernel(x, c, ctx, c_ctx, ada_w, ada_b, ln_g, ln_b, ab_w_in, ab_w_out, diff_lam, diff_norm_g,
           sgu_ln_g, sgu_ln_b, sgu_w, sgu_b, pool_w_in, pool_w_grp, pool_scale, pool_w_out,
           peer_wq, peer_keys, peer_u, peer_v):
    bsz, s_len, d = x.shape
    c_len = ctx.shape[1]
    depth = ada_w.shape[0]
    qk_w = ab_w_in.shape[2] // 4
    alpha = (2.0 * depth) ** 0.25
    last_ctx_read = 2 * ((depth - 1) // 2)
    cos, sin = _rope_tables_fm(s_len // GRID_W)
    c_tile = min(TOKEN_TILE, bsz * c_len)
    cos_c = jnp.ones((HEAD_DIM // 2, c_tile), F32)
    sin_c = jnp.zeros((HEAD_DIM // 2, c_tile), F32)

    xs = x.reshape(bsz * s_len, d).T
    cs = ctx.reshape(bsz * c_len, d).T

    cond = jnp.zeros((d, LANE), F32)
    cond = cond.at[:, :bsz].set(jax.nn.silu(c).T).at[:, bsz].set(jax.nn.silu(c_ctx))
    cond_t = cond.T.astype(BF16)
    u_bf = peer_u.astype(BF16)
    vt_bf = jnp.transpose(peer_v, (0, 2, 1)).astype(BF16)
    mods_x, mods_c = [], []
    for i in range(depth):
        ada = matmul_fm([(cond_t, ada_w[i])], F32).T + ada_b[i][:, None]
        mods_x.append([ada[k * d:(k + 1) * d, :bsz] for k in range(6)])
        mods_c.append([ada[k * d:(k + 1) * d, bsz:bsz + 1] for k in range(6)])

    def uses_ctx(i):
        return i < depth and (i < last_ctx_read or i % 2 == 0)

    _, hx = ln_mod_fm(xs, mod=(mods_x[0][0], mods_x[0][1]))
    hc = ln_mod_fm(cs, mod=(mods_c[0][0], mods_c[0][1]))[1] if uses_ctx(0) else None

    for i in range(depth):
        j = i // 2
        even = i % 2 == 0
        ctx_out = i < last_ctx_read
        last = i == depth - 1
        sh1, sc1, g1, sh2, sc2, g2 = mods_x[i]
        csh1, csc1, cg1, csh2, csc2, cg2 = mods_c[i]

        if even:
            lam_init = 0.8 - 0.6 * math.exp(-0.3 * i)
            lq1, lk1, lq2, lk2 = diff_lam[j].astype(F32)
            lam = (jnp.exp(jnp.sum(lq1 * lk1)) - jnp.exp(jnp.sum(lq2 * lk2)) + lam_init).reshape(1, 1)
            w_in_t = ab_w_in[j].T.astype(BF16)
            w_out_t = ab_w_out[j].T.astype(BF16)
            norm_g = diff_norm_g[j][:, None]
            sgu_args = (sgu_ln_g[j][:, None], sgu_ln_b[j][:, None],
                        jnp.transpose(sgu_w[j], (0, 2, 1)).astype(BF16), sgu_b[j][:, None, :])
            q_x, k_x, v_x, g_x = ab_in_proj(hx, w_in_t, cos, sin, *sgu_args)
            q_c, k_c, v_c, g_c = ab_in_proj(hc, w_in_t, cos_c, sin_c, *sgu_args)
            k_all, v_all = _kv_all(k_c, k_x, v_c, v_x, bsz)
            a_x = diff_attention(q_x, k_all, v_all, lam, norm_g, lam_init, bsz,
                                 tq=min(ATTN_Q_TILE, s_len), tk=_pick_tk(c_len + s_len))
            yx = matmul_fm([(w_out_t[:, :qk_w], a_x), (w_out_t[:, qk_w:], g_x)], F32)
            if ctx_out:
                a_c = diff_attention(q_c, k_c, v_c, lam, norm_g, lam_init, bsz, tq=c_len, tk=_pick_tk(c_len))
                yc = matmul_fm([(w_out_t[:, :qk_w], a_c), (w_out_t[:, qk_w:], g_c)], F32)
        else:
            p_in_t = pool_w_in[j].T.astype(BF16)
            wg_t = jnp.transpose(pool_w_grp[j], (0, 2, 1)).astype(BF16)
            p_out_t = pool_w_out[j].T.astype(BF16)
            scale = pool_scale[j][:, None]
            yx = pool_mixer_fm(matmul_fm([(p_in_t, hx)], F32), wg_t, scale, p_out_t, s_len)
            if ctx_out:
                yc = pool_mixer_fm(matmul_fm([(p_in_t, hc)], F32), wg_t, scale, p_out_t, c_len)

        wq_t = peer_wq[i].T.astype(BF16)
        keys_bf = peer_keys[i].reshape(2 * P_HEADS, N_KEYS, -1).astype(BF16)
        next_mod_c = (mods_c[i + 1][0], mods_c[i + 1][1]) if uses_ctx(i + 1) else None
        if ctx_out:
            cs, hc = ln_mod_fm(cs, resid=(yc, cg1, ln_g[i, 0], ln_b[i, 0]), mod=(csh2, csc2), alpha=alpha)
            fc = peer_fm(hc, wq_t, keys_bf, u_bf, vt_bf, i)
            if uses_ctx(i + 1):
                cs, hc = ln_mod_fm(cs, resid=(fc, cg2, ln_g[i, 1], ln_b[i, 1]), mod=next_mod_c, alpha=alpha)
        elif next_mod_c is not None:
            hc = ln_mod_fm(cs, mod=next_mod_c)[1]

        xs, hx = ln_mod_fm(xs, resid=(yx, g1, ln_g[i, 0], ln_b[i, 0]), mod=(sh2, sc2), alpha=alpha)
        fx = peer_fm(hx, wq_t, keys_bf, u_bf, vt_bf, i)
        next_mod_x = None if last else (mods_x[i + 1][0], mods_x[i + 1][1])
        xs, hx = ln_mod_fm(xs, resid=(fx, g2, ln_g[i, 1], ln_b[i, 1]), mod=next_mod_x, alpha=alpha)

    return xs.T.reshape(bsz, s_len, d)
```

```python
import functools
import math

import jax
import jax.numpy as jnp
from jax import lax
from jax.experimental import pallas as pl
from jax.experimental.pallas import tpu as pltpu

F32 = jnp.float32
BF16 = jnp.bfloat16
U32 = jnp.uint32

GRID_W = 64
LN_EPS = 1e-5
HEAD_DIM = 64
V_DIM = 2 * HEAD_DIM
ATTN_SCALE = HEAD_DIM ** -0.5
Q_SCALE_LOG2 = ATTN_SCALE * math.log2(math.e)
ROPE_BASE = 10000.0
ROPE_AXIS_PAIRS = HEAD_DIM // 4
CHUNK = 128
SGU_GROUPS = 4
POOL_WINDOWS = (2, 4, 8, 16)
P_HEADS = 8
N_KEYS = 128
P_TOPK = 16

LANE = 128
SUBLANE = 8
BF16_ROWS = 16
VMEM_LIMIT = 56 * 1024 * 1024
TOKEN_TILE = 1024


def _params(*sem):
    return pltpu.CompilerParams(dimension_semantics=sem, vmem_limit_bytes=VMEM_LIMIT)


def _gelu_tanh(x):
    c1 = math.sqrt(2.0 / math.pi)
    c2 = c1 * 0.044715
    inner = x * (c1 + c2 * (x * x))
    return x * (0.5 + 0.5 * jnp.tanh(inner))


def _mm_kernel(*refs, n_pairs, has_bias):
    o_ref = refs[-1]
    acc = None
    for i in range(n_pairs):
        part = jnp.dot(refs[2 * i][...].astype(BF16), refs[2 * i + 1][...].astype(BF16),
                       preferred_element_type=F32)
        acc = part if acc is None else acc + part
    if has_bias:
        acc = acc + refs[2 * n_pairs][...]
    o_ref[...] = acc.astype(o_ref.dtype)


def matmul_fm(pairs, out_dtype, bias=None, tn=TOKEN_TILE, tm=None):
    m = pairs[0][0].shape[0]
    n = pairs[0][1].shape[1]
    tn = min(tn, n)
    tm = m if tm is None else tm
    args, specs = [], []
    for w, x in pairs:
        k = w.shape[1]
        args += [w, x]
        specs += [pl.BlockSpec((tm, k), lambda j, i: (i, 0)),
                  pl.BlockSpec((k, tn), lambda j, i: (0, j))]
    if bias is not None:
        args.append(bias)
        specs.append(pl.BlockSpec((tm, 1), lambda j, i: (i, 0)))
    return pl.pallas_call(
        functools.partial(_mm_kernel, n_pairs=len(pairs), has_bias=bias is not None),
        grid=(n // tn, m // tm),
        in_specs=specs,
        out_specs=pl.BlockSpec((tm, tn), lambda j, i: (i, j)),
        out_shape=jax.ShapeDtypeStruct((m, n), out_dtype),
        compiler_params=_params("parallel", "parallel"),
        name="matmul_fm",
    )(*args)


def _ln_mod_kernel(*refs, alpha, has_resid, emit_x, emit_h):
    it = iter(refs)
    x = next(it)[...]
    if has_resid:
        y_ref, gate_ref, lng_ref, lnb_ref = next(it), next(it), next(it), next(it)
    if emit_h:
        shift_ref, scale_ref = next(it), next(it)
    if has_resid:
        z = alpha * x + gate_ref[...] * y_ref[...]
        mu = jnp.mean(z, axis=0, keepdims=True)
        zc = z - mu
        var = jnp.mean(zc * zc, axis=0, keepdims=True)
        x = zc * lax.rsqrt(var + LN_EPS) * lng_ref[...] + lnb_ref[...]
        if emit_x:
            next(it)[...] = x
    if emit_h:
        next(it)[...] = (x * (1.0 + scale_ref[...]) + shift_ref[...]).astype(BF16)


def ln_mod_fm(xs, *, resid=None, mod=None, alpha=1.0, emit_x=True, tn=TOKEN_TILE):
    d, n = xs.shape
    tn = min(tn, n)
    n_seg = (resid[1] if resid is not None else mod[0]).shape[1]
    tps = (n // n_seg) // tn
    tok = pl.BlockSpec((d, tn), lambda j: (0, j))
    seg = pl.BlockSpec((None, d, 1), lambda j: (j // tps, 0, 0))
    col = pl.BlockSpec((d, 1), lambda j: (0, 0))
    args, specs, out_shape, out_specs = [xs], [tok], [], []
    if resid is not None:
        y, gate, ln_g, ln_b = resid
        args += [y, gate.T[:, :, None], ln_g[:, None], ln_b[:, None]]
        specs += [tok, seg, col, col]
        if emit_x:
            out_shape.append(jax.ShapeDtypeStruct((d, n), F32))
            out_specs.append(tok)
    if mod is not None:
        args += [mod[0].T[:, :, None], mod[1].T[:, :, None]]
        specs += [seg, seg]
        out_shape.append(jax.ShapeDtypeStruct((d, n), BF16))
        out_specs.append(tok)
    outs = pl.pallas_call(
        functools.partial(_ln_mod_kernel, alpha=alpha, has_resid=resid is not None,
                          emit_x=emit_x and resid is not None, emit_h=mod is not None),
        grid=(n // tn,),
        in_specs=specs,
        out_specs=out_specs,
        out_shape=out_shape,
        compiler_params=_params("parallel"),
        name="ln_mod",
    )(*args)
    outs = list(outs)
    x_new = outs.pop(0) if (resid is not None and emit_x) else None
    h = outs.pop(0) if mod is not None else None
    return x_new, h


ATTN_Q_TILE = 1024
V_ROWS = V_DIM + BF16_ROWS


def _attn_kernel(lam_ref, q_ref, k_ref, v_ref, ones_ref, g_ref, o_ref, qp_ref, m_ref, acc_ref,
                 *, tq, tk, n_chunks, out_scale):
    q = q_ref[...]
    row = lax.broadcasted_iota(jnp.int32, q.shape, 0)
    zero = jnp.zeros_like(q)
    qp_ref[:, :tq] = jnp.where(row < HEAD_DIM, q, zero)
    qp_ref[:, tq:] = jnp.where(row >= HEAD_DIM, q, zero)
    m_ref[...] = jnp.full(m_ref.shape, -jnp.inf, F32)
    acc_ref[...] = jnp.zeros(acc_ref.shape, F32)

    n_groups = 2
    items = [(c, g) for c in range(n_chunks) for g in range(n_groups)]

    def lanes_of(item):
        return slice(item[1] * tq, (item[1] + 1) * tq)

    def scores(item):
        return jnp.dot(k_ref[item[0] * tk:(item[0] + 1) * tk, :], qp_ref[:, lanes_of(item)],
                       preferred_element_type=F32)

    def running_max(item, s):
        m_prev = m_ref[:, lanes_of(item)]
        m_new = jnp.maximum(m_prev, jnp.max(s, axis=0, keepdims=True))
        m_ref[:, lanes_of(item)] = m_new
        return m_new, jnp.exp2(m_prev - m_new)

    assert n_groups >= 2
    n_items = len(items)
    s_buf = {0: scores(items[0])}
    if n_items > 1:
        s_buf[1] = scores(items[1])
    stats = {0: running_max(items[0], s_buf[0])}
    for idx, item in enumerate(items):
        if idx + 2 < n_items:
            s_buf[idx + 2] = scores(items[idx + 2])
        if idx + 1 < n_items:
            stats[idx + 1] = running_max(items[idx + 1], s_buf[idx + 1])
        m_new, alpha = stats.pop(idx)
        p = jnp.exp2(s_buf.pop(idx) - m_new).astype(BF16)
        lanes = lanes_of(item)
        v_ext = jnp.concatenate([v_ref[:, item[0] * tk:(item[0] + 1) * tk], ones_ref[...]], axis=0)
        acc_ref[:, lanes] = acc_ref[:, lanes] * alpha + jnp.dot(v_ext, p, preferred_element_type=F32)

    lam = lam_ref[0, 0]
    acc = acc_ref[...]
    inv = 1.0 / acc[V_DIM:V_DIM + 1, :]
    o = acc[:V_DIM, :tq] * inv[:, :tq] - lam * (acc[:V_DIM, tq:] * inv[:, tq:])
    ms = jnp.mean(o * o, axis=0, keepdims=True)
    o = o * lax.rsqrt(ms + LN_EPS) * g_ref[...] * out_scale
    o_ref[...] = o.astype(o_ref.dtype)


def diff_attention(q_fm, k_tok, v_fm, lam, norm_g, lam_init, n_batch, tq, tk):
    n_heads = q_fm.shape[0] // V_DIM
    sq = q_fm.shape[1] // n_batch
    nq = sq // tq
    l_keys = k_tok.shape[0] // n_batch
    nc = l_keys // tk
    ones = jnp.zeros((BF16_ROWS, tk), BF16).at[0].set(1.0)
    return pl.pallas_call(
        functools.partial(_attn_kernel, tq=tq, tk=tk, n_chunks=nc, out_scale=1.0 - lam_init),
        grid=(n_batch, n_heads, nq),
        in_specs=[
            pl.BlockSpec(memory_space=pltpu.SMEM),
            pl.BlockSpec((V_DIM, tq), lambda b, h, i: (h, b * nq + i)),
            pl.BlockSpec((l_keys, V_DIM), lambda b, h, i: (b, h)),
            pl.BlockSpec((V_DIM, l_keys), lambda b, h, i: (h, b)),
            pl.BlockSpec((BF16_ROWS, tk), lambda b, h, i: (0, 0)),
            pl.BlockSpec((V_DIM, 1), lambda b, h, i: (0, 0)),
        ],
        out_specs=pl.BlockSpec((V_DIM, tq), lambda b, h, i: (h, b * nq + i)),
        out_shape=jax.ShapeDtypeStruct(q_fm.shape, BF16),
        scratch_shapes=[
            pltpu.VMEM((V_DIM, 2 * tq), BF16),
            pltpu.VMEM((1, 2 * tq), F32),
            pltpu.VMEM((V_ROWS, 2 * tq), F32),
        ],
        compiler_params=_params("parallel", "parallel", "parallel"),
        name="diff_attention",
    )(lam, q_fm, k_tok, v_fm, ones, norm_g)


def _ab_in_kernel(h_ref, w_ref, cos_ref, sin_ref, lng_ref, lnb_ref, wst_ref, bs_ref,
                  q_ref, k_ref, v_ref, g_ref, *, qk_w, sgu_half):
    p = jnp.dot(w_ref[...], h_ref[...], preferred_element_type=F32)
    cs, sn = cos_ref[...], sin_ref[...]
    half = HEAD_DIM // 2
    k_rows = []
    for blk in range(qk_w // HEAD_DIM):
        r = blk * HEAD_DIM
        a, b = p[r:r + half], p[r + half:r + HEAD_DIM]
        q_ref[r:r + half, :] = ((a * cs - b * sn) * Q_SCALE_LOG2).astype(BF16)
        q_ref[r + half:r + HEAD_DIM, :] = ((a * sn + b * cs) * Q_SCALE_LOG2).astype(BF16)
        a, b = p[qk_w + r:qk_w + r + half], p[qk_w + r + half:qk_w + r + HEAD_DIM]
        k_rows += [a * cs - b * sn, a * sn + b * cs]
    k_ref[...] = jnp.concatenate(k_rows, axis=0).T.astype(BF16)
    v_ref[...] = p[2 * qk_w:3 * qk_w].astype(BF16)

    gu = p[3 * qk_w:3 * qk_w + sgu_half]
    v = _gelu_tanh(p[3 * qk_w + sgu_half:])
    mu = jnp.mean(v, axis=0, keepdims=True)
    vc = v - mu
    var = jnp.mean(vc * vc, axis=0, keepdims=True)
    vb = (vc * lax.rsqrt(var + LN_EPS) * lng_ref[...] + lnb_ref[...]).astype(BF16)
    gc = sgu_half // SGU_GROUPS
    for g in range(SGU_GROUPS):
        for j in range(vb.shape[1] // CHUNK):
            rows, cols = slice(g * gc, (g + 1) * gc), slice(j * CHUNK, (j + 1) * CHUNK)
            s = jnp.dot(vb[rows, cols], wst_ref[g], preferred_element_type=F32) + bs_ref[g]
            g_ref[rows, cols] = (_gelu_tanh(gu[rows, cols]) * s).astype(BF16)


def ab_in_proj(h_bf, w_in_t, cos_tab, sin_tab, ln_g, ln_b, ws_t, bs, tn=TOKEN_TILE):
    d, n = h_bf.shape
    tn = min(tn, n)
    qk_w = w_in_t.shape[0] // 4
    sgu_half = (w_in_t.shape[0] - 3 * qk_w) // 2
    n_pos = cos_tab.shape[1] // tn
    tok = lambda rows: pl.BlockSpec((rows, tn), lambda j: (0, j))
    pos = pl.BlockSpec((HEAD_DIM // 2, tn), lambda j: (0, j % n_pos))
    return pl.pallas_call(
        functools.partial(_ab_in_kernel, qk_w=qk_w, sgu_half=sgu_half),
        grid=(n // tn,),
        in_specs=[
            tok(d),
            pl.BlockSpec(w_in_t.shape, lambda j: (0, 0)),
            pos, pos,
            pl.BlockSpec((sgu_half, 1), lambda j: (0, 0)),
            pl.BlockSpec((sgu_half, 1), lambda j: (0, 0)),
            pl.BlockSpec((SGU_GROUPS, CHUNK, CHUNK), lambda j: (0, 0, 0)),
            pl.BlockSpec((SGU_GROUPS, 1, CHUNK), lambda j: (0, 0, 0)),
        ],
        out_specs=[tok(qk_w), pl.BlockSpec((tn, qk_w), lambda j: (j, 0)), tok(qk_w), tok(sgu_half)],
        out_shape=[jax.ShapeDtypeStruct((qk_w, n), BF16), jax.ShapeDtypeStruct((n, qk_w), BF16),
                   jax.ShapeDtypeStruct((qk_w, n), BF16), jax.ShapeDtypeStruct((sgu_half, n), BF16)],
        compiler_params=_params("parallel"),
        name="ab_in_proj",
    )(h_bf, w_in_t, cos_tab, sin_tab, ln_g, ln_b, ws_t, bs)


POOL_TILE = 512
POOL_HALO = 128


def _pool_kernel(left_ref, mid_ref, right_ref, wg_ref, scale_ref, wout_ref, o_ref, *, seq_len, tile):
    j = pl.program_id(0)
    t0 = j * tile
    seq_start = (t0 // seq_len) * seq_len
    win = tile + 2 * POOL_HALO
    h_all = jnp.concatenate([left_ref[...], mid_ref[...], right_ref[...]], axis=1)
    h_hi = h_all.astype(BF16)
    h_lo = (h_all - h_hi.astype(F32)).astype(BF16)
    tau = t0 - POOL_HALO + lax.broadcasted_iota(jnp.int32, (win, tile), 0)
    t = t0 + lax.broadcasted_iota(jnp.int32, (win, tile), 1)
    gc = mid_ref.shape[0] // len(POOL_WINDOWS)
    outs = []
    for g, w in enumerate(POOL_WINDOWS):
        lo = jnp.maximum(t - w // 2, seq_start)
        hi = jnp.minimum(t + (w - w // 2), seq_start + seq_len)
        band = jnp.where((tau >= lo) & (tau < hi), 1.0, 0.0).astype(BF16)
        cnt = (hi - lo)[0:1, :].astype(F32)
        sl = slice(g * gc, (g + 1) * gc)
        ssum = (jnp.dot(h_hi[sl], band, preferred_element_type=F32)
                + jnp.dot(h_lo[sl], band, preferred_element_type=F32))
        m = ssum / cnt - mid_ref[sl, :]
        m = jnp.dot(wg_ref[g], m.astype(BF16), preferred_element_type=F32)
        outs.append(m)
    m_all = (jnp.concatenate(outs, axis=0) * scale_ref[...]).astype(BF16)
    o_ref[...] = jnp.dot(wout_ref[...], m_all, preferred_element_type=F32)


def pool_mixer_fm(h, wg_t, scale, wout_t, seq_len):
    c, n = h.shape
    tile = min(POOL_TILE, seq_len)
    nt = n // tile
    r = tile // POOL_HALO
    nh = n // POOL_HALO
    return pl.pallas_call(
        functools.partial(_pool_kernel, seq_len=seq_len, tile=tile),
        grid=(nt,),
        in_specs=[
            pl.BlockSpec((c, POOL_HALO), lambda j: (0, jnp.maximum(j * r - 1, 0))),
            pl.BlockSpec((c, tile), lambda j: (0, j)),
            pl.BlockSpec((c, POOL_HALO), lambda j: (0, jnp.minimum(j * r + r, nh - 1))),
            pl.BlockSpec(wg_t.shape, lambda j: (0, 0, 0)),
            pl.BlockSpec((c, 1), lambda j: (0, 0)),
            pl.BlockSpec(wout_t.shape, lambda j: (0, 0)),
        ],
        out_specs=pl.BlockSpec((wout_t.shape[0], tile), lambda j: (0, j)),
        out_shape=jax.ShapeDtypeStruct((wout_t.shape[0], n), F32),
        compiler_params=_params("parallel"),
        name="pool_mixer",
    )(h, h, h, wg_t, scale, wout_t)


def _sort16_desc(xs):
    xs = list(xs)
    n = len(xs)
    k = 2
    while k <= n:
        j = k // 2
        while j >= 1:
            for i in range(n):
                l = i ^ j
                if l > i:
                    hi, lo = jnp.maximum(xs[i], xs[l]), jnp.minimum(xs[i], xs[l])
                    if (i & k) == 0:
                        xs[i], xs[l] = hi, lo
                    else:
                        xs[i], xs[l] = lo, hi
            j //= 2
        k *= 2
    return xs


def _bitonic_merge_desc(xs):
    xs = list(xs)
    n = len(xs)
    j = n // 2
    while j >= 1:
        for i in range(n):
            l = i ^ j
            if l > i:
                xs[i], xs[l] = jnp.maximum(xs[i], xs[l]), jnp.minimum(xs[i], xs[l])
        j //= 2
    return xs


def _top16_of_two(a, b):
    n = len(a)
    return _bitonic_merge_desc([jnp.maximum(a[i], b[n - 1 - i]) for i in range(n)])


def _top16_desc(vals):
    groups = [_sort16_desc(vals[i:i + P_TOPK]) for i in range(0, len(vals), P_TOPK)]
    while len(groups) > 1:
        groups = [_top16_of_two(groups[i], groups[i + 1]) for i in range(0, len(groups), 2)]
    return groups[0]


def _peer_route_kernel(h_ref, wq_ref, keys_ref, lrow_ref, e1_ref, rank2_ref, e2_ref, s_s):
    q = jnp.dot(wq_ref[...], h_ref[...], preferred_element_type=F32).astype(BF16)
    n_half = h_ref.shape[1] // LANE

    def sheet(p, rows):
        return jnp.concatenate([s_s[half, p, rows, :] for half in range(n_half)], axis=1)

    for hp in range(2 * P_HEADS):
        h, p = divmod(hp, 2)
        s = jnp.dot(keys_ref[hp], q[hp * N_KEYS:(hp + 1) * N_KEYS, :], preferred_element_type=F32)
        for half in range(n_half):
            s_s[half, p, pl.ds(h, N_KEYS, stride=P_HEADS), :] = s[:, half * LANE:(half + 1) * LANE]

    s1 = [sheet(0, slice(k * P_HEADS, (k + 1) * P_HEADS)) for k in range(N_KEYS)]
    s2 = [sheet(1, slice(k * P_HEADS, (k + 1) * P_HEADS)) for k in range(N_KEYS)]
    a = _top16_desc(s1)
    b = _top16_desc(s2)
    cand = [[a[r1] + b[r2] for r2 in range(P_TOPK)] for r1 in range(P_TOPK)]
    rows = [list(r) for r in cand]
    while len(rows) > 1:
        rows = [_top16_of_two(rows[i], rows[i + 1]) for i in range(0, len(rows), 2)]
    best = rows[0]
    thr = best[P_TOPK - 1]
    z = None
    for r in range(P_TOPK):
        e = jnp.exp(best[r] - best[0])
        z = e if z is None else z + e
    inv_z = 1.0 / z
    counts = []
    for r1 in range(P_TOPK):
        cnt = jnp.zeros_like(thr)
        for r2 in range(P_TOPK):
            cnt = cnt + jnp.where(cand[r1][r2] >= thr, 1.0, 0.0)
        counts.append(cnt)

    def per_key(k, carry):
        base = pl.multiple_of(k * P_HEADS, P_HEADS)
        s1k = sheet(0, pl.ds(base, P_HEADS))
        s2k = sheet(1, pl.ds(base, P_HEADS))
        lrow = jnp.zeros_like(thr)
        rank = jnp.zeros_like(thr)
        for r in range(P_TOPK):
            lrow = jnp.where(s1k == a[r], counts[r], lrow)
            rank = jnp.where(b[r] > s2k, float(r + 1), rank)
        lrow_ref[k] = lrow
        e1_ref[k] = jnp.exp(s1k - a[0])
        rank2_ref[k] = rank
        e2_ref[k] = jnp.exp(s2k - b[0]) * inv_z
        return carry

    lax.fori_loop(0, N_KEYS, per_key, 0, unroll=8)


def peer_route(h_bf, wq_t, keys_bf, tn=2 * LANE):
    d, n = h_bf.shape
    rows = N_KEYS * P_HEADS
    spec = pl.BlockSpec((N_KEYS, P_HEADS, tn), lambda j: (0, 0, j))
    sheet = lambda dt: jax.ShapeDtypeStruct((N_KEYS, P_HEADS, n), dt)
    return pl.pallas_call(
        _peer_route_kernel,
        grid=(n // tn,),
        in_specs=[
            pl.BlockSpec((d, tn), lambda j: (0, j)),
            pl.BlockSpec(wq_t.shape, lambda j: (0, 0)),
            pl.BlockSpec(keys_bf.shape, lambda j: (0, 0, 0)),
        ],
        out_specs=[spec] * 4,
        out_shape=[sheet(F32)] * 4,
        scratch_shapes=[pltpu.VMEM((tn // LANE, 2, rows, LANE), F32)],
        compiler_params=_params("parallel"),
        name="peer_route",
    )(h_bf, wq_t, keys_bf)


PEER_TOKEN_TILE = 1024
PEER_I1_TILE = 8


def _peer_dense_kernel(h_ref, u_ref, vt_ref, lrow_ref, e1_ref, rank2_ref, e2_ref, o_ref, w_s, *, i1_tile):
    e = pl.program_id(1)

    @pl.when(e == 0)
    def _():
        o_ref[...] = jnp.zeros(o_ref.shape, F32)

    act = _gelu_tanh(jnp.dot(u_ref[...], h_ref[...], preferred_element_type=F32).astype(BF16))
    tn = act.shape[1]
    zero = jnp.zeros((), BF16)
    for il in range(i1_tile):
        lr = lrow_ref[il * P_HEADS:(il + 1) * P_HEADS, :]
        e1 = e1_ref[il * P_HEADS:(il + 1) * P_HEADS, :]
        gate = None
        for h in range(P_HEADS):
            lb = jnp.broadcast_to(lr[h:h + 1, :], (N_KEYS, tn)).astype(BF16)
            eb = jnp.broadcast_to(e1[h:h + 1, :], (N_KEYS, tn)).astype(BF16)
            keys = slice(h * N_KEYS, (h + 1) * N_KEYS)
            term = jnp.where(rank2_ref[keys, :] < lb, e2_ref[keys, :], zero) * eb
            gate = term if gate is None else gate + term
        w_s[il * N_KEYS:(il + 1) * N_KEYS, :] = gate * act[il * N_KEYS:(il + 1) * N_KEYS, :]
    o_ref[...] += jnp.dot(vt_ref[...], w_s[...], preferred_element_type=F32)


def peer_dense(h_bf, u_bf, vt_bf, layer, lrow, e1, rank2, e2, tn=PEER_TOKEN_TILE, i1_tile=PEER_I1_TILE):
    d, n = h_bf.shape
    n_exp = u_bf.shape[1]
    tn = min(tn, n)
    et = i1_tile * N_KEYS
    table = pl.BlockSpec((P_HEADS * N_KEYS, tn), lambda j, e: (0, j))
    return pl.pallas_call(
        functools.partial(_peer_dense_kernel, i1_tile=i1_tile),
        grid=(n // tn, n_exp // et),
        in_specs=[
            pl.BlockSpec((d, tn), lambda j, e: (0, j)),
            pl.BlockSpec((None, et, d), lambda j, e: (layer, e, 0)),
            pl.BlockSpec((None, d, et), lambda j, e: (layer, 0, e)),
            pl.BlockSpec((i1_tile * P_HEADS, tn), lambda j, e: (e, j)),
            pl.BlockSpec((i1_tile * P_HEADS, tn), lambda j, e: (e, j)),
            table, table,
        ],
        out_specs=pl.BlockSpec((d, tn), lambda j, e: (0, j)),
        out_shape=jax.ShapeDtypeStruct((d, n), F32),
        scratch_shapes=[pltpu.VMEM((et, tn), BF16)],
        compiler_params=_params("parallel", "arbitrary"),
        name="peer_dense",
    )(h_bf, u_bf, vt_bf, lrow, e1, rank2, e2)


def peer_fm(h_bf, wq_t, keys_bf, u_bf, vt_bf, layer):
    lrow, e1, rank2, e2 = peer_route(h_bf, wq_t, keys_bf)
    rows, n = N_KEYS * P_HEADS, h_bf.shape[1]
    head_major = lambda t: jnp.transpose(t, (1, 0, 2)).reshape(rows, n).astype(BF16)
    return peer_dense(h_bf, u_bf, vt_bf, layer, lrow.reshape(rows, n), e1.reshape(rows, n),
                      head_major(rank2), head_major(e2))


def _rope_tables_fm(rows):
    row = jnp.repeat(jnp.arange(rows, dtype=F32), GRID_W)
    col = jnp.tile(jnp.arange(GRID_W, dtype=F32), rows)
    inv = ROPE_BASE ** (-jnp.arange(ROPE_AXIS_PAIRS, dtype=F32) / ROPE_AXIS_PAIRS)
    ang = jnp.concatenate([inv[:, None] * row[None, :], inv[:, None] * col[None, :]], axis=0)
    return jnp.cos(ang), jnp.sin(ang)


def _kv_all(kc_tok, kx_tok, vc_fm, vx_fm, n_batch):
    c_len, s_len = kc_tok.shape[0] // n_batch, kx_tok.shape[0] // n_batch
    k_parts, v_parts = [], []
    for b in range(n_batch):
        k_parts += [kc_tok[b * c_len:(b + 1) * c_len], kx_tok[b * s_len:(b + 1) * s_len]]
        v_parts += [vc_fm[:, b * c_len:(b + 1) * c_len], vx_fm[:, b * s_len:(b + 1) * s_len]]
    return jnp.concatenate(k_parts, axis=0), jnp.concatenate(v_parts, axis=1)


def _pick_tk(l):
    for tk in (768, 512, 256, 128):
        if l % tk == 0:
            return tk
    raise ValueError(f"unsupported key length {l}")


def kernel(x, c, ctx, c_ctx, ada_w, ada_b, ln_g, ln_b, ab_w_in, ab_w_out, diff_lam, diff_norm_g,
           sgu_ln_g, sgu_ln_b, sgu_w, sgu_b, pool_w_in, pool_w_grp, pool_scale, pool_w_out,
           peer_wq, peer_keys, peer_u, peer_v):
    bsz, s_len, d = x.shape
    c_len = ctx.shape[1]
    depth = ada_w.shape[0]
    qk_w = ab_w_in.shape[2] // 4
    alpha = (2.0 * depth) ** 0.25
    last_ctx_read = 2 * ((depth - 1) // 2)
    cos, sin = _rope_tables_fm(s_len // GRID_W)
    c_tile = min(TOKEN_TILE, bsz * c_len)
    cos_c = jnp.ones((HEAD_DIM // 2, c_tile), F32)
    sin_c = jnp.zeros((HEAD_DIM // 2, c_tile), F32)

    xs = x.reshape(bsz * s_len, d).T
    cs = ctx.reshape(bsz * c_len, d).T

    cond = jnp.zeros((d, LANE), F32)
    cond = cond.at[:, :bsz].set(jax.nn.silu(c).T).at[:, bsz].set(jax.nn.silu(c_ctx))
    cond_t = cond.T.astype(BF16)
    u_bf = peer_u.astype(BF16)
    vt_bf = jnp.transpose(peer_v, (0, 2, 1)).astype(BF16)
    mods_x, mods_c = [], []
    for i in range(depth):
        ada = matmul_fm([(cond_t, ada_w[i])], F32).T + ada_b[i][:, None]
        mods_x.append([ada[k * d:(k + 1) * d, :bsz] for k in range(6)])
        mods_c.append([ada[k * d:(k + 1) * d, bsz:bsz + 1] for k in range(6)])

    def uses_ctx(i):
        return i < depth and (i < last_ctx_read or i % 2 == 0)

    _, hx = ln_mod_fm(xs, mod=(mods_x[0][0], mods_x[0][1]))
    hc = ln_mod_fm(cs, mod=(mods_c[0][0], mods_c[0][1]))[1] if uses_ctx(0) else None

    for i in range(depth):
        j = i // 2
        even = i % 2 == 0
        ctx_out = i < last_ctx_read
        last = i == depth - 1
        sh1, sc1, g1, sh2, sc2, g2 = mods_x[i]
        csh1, csc1, cg1, csh2, csc2, cg2 = mods_c[i]

        if even:
            lam_init = 0.8 - 0.6 * math.exp(-0.3 * i)
            lq1, lk1, lq2, lk2 = diff_lam[j].astype(F32)
            lam = (jnp.exp(jnp.sum(lq1 * lk1)) - jnp.exp(jnp.sum(lq2 * lk2)) + lam_init).reshape(1, 1)
            w_in_t = ab_w_in[j].T.astype(BF16)
            w_out_t = ab_w_out[j].T.astype(BF16)
            norm_g = diff_norm_g[j][:, None]
            sgu_args = (sgu_ln_g[j][:, None], sgu_ln_b[j][:, None],
                        jnp.transpose(sgu_w[j], (0, 2, 1)).astype(BF16), sgu_b[j][:, None, :])
            q_x, k_x, v_x, g_x = ab_in_proj(hx, w_in_t, cos, sin, *sgu_args)
            q_c, k_c, v_c, g_c = ab_in_proj(hc, w_in_t, cos_c, sin_c, *sgu_args)
            k_all, v_all = _kv_all(k_c, k_x, v_c, v_x, bsz)
            a_x = diff_attention(q_x, k_all, v_all, lam, norm_g, lam_init, bsz,
                                 tq=min(ATTN_Q_TILE, s_len), tk=_pick_tk(c_len + s_len))
            yx = matmul_fm([(w_out_t[:, :qk_w], a_x), (w_out_t[:, qk_w:], g_x)], F32)
            if ctx_out:
                a_c = diff_attention(q_c, k_c, v_c, lam, norm_g, lam_init, bsz, tq=c_len, tk=_pick_tk(c_len))
                yc = matmul_fm([(w_out_t[:, :qk_w], a_c), (w_out_t[:, qk_w:], g_c)], F32)
        else:
            p_in_t = pool_w_in[j].T.astype(BF16)
            wg_t = jnp.transpose(pool_w_grp[j], (0, 2, 1)).astype(BF16)
            p_out_t = pool_w_out[j].T.astype(BF16)
            scale = pool_scale[j][:, None]
            yx = pool_mixer_fm(matmul_fm([(p_in_t, hx)], F32), wg_t, scale, p_out_t, s_len)
            if ctx_out:
                yc = pool_mixer_fm(matmul_fm([(p_in_t, hc)], F32), wg_t, scale, p_out_t, c_len)

        wq_t = peer_wq[i].T.astype(BF16)
        keys_bf = peer_keys[i].reshape(2 * P_HEADS, N_KEYS, -1).astype(BF16)
        next_mod_c = (mods_c[i + 1][0], mods_c[i + 1][1]) if uses_ctx(i + 1) else None
        if ctx_out:
            cs, hc = ln_mod_fm(cs, resid=(yc, cg1, ln_g[i, 0], ln_b[i, 0]), mod=(csh2, csc2), alpha=alpha)
            fc = peer_fm(hc, wq_t, keys_bf, u_bf, vt_bf, i)
            if uses_ctx(i + 1):
                cs, hc = ln_mod_fm(cs, resid=(fc, cg2, ln_g[i, 1], ln_b[i, 1]), mod=next_mod_c, alpha=alpha)
        elif next_mod_c is not None:
            hc = ln_mod_fm(cs, mod=next_mod_c)[1]

        xs, hx = ln_mod_fm(xs, resid=(yx, g1, ln_g[i, 0], ln_b[i, 0]), mod=(sh2, sc2), alpha=alpha)
        fx = peer_fm(hx, wq_t, keys_bf, u_bf, vt_bf, i)
        next_mod_x = None if last else (mods_x[i + 1][0], mods_x[i + 1][1])
        xs, hx = ln_mod_fm(xs, resid=(fx, g2, ln_g[i, 1], ln_b[i, 1]), mod=next_mod_x, alpha=alpha)

    return xs.T.reshape(bsz, s_len, d)
```

```python
import functools
import math

import jax
import jax.numpy as jnp
from jax import lax
from jax.experimental import pallas as pl
from jax.experimental.pallas import tpu as pltpu

F32 = jnp.float32
BF16 = jnp.bfloat16
U32 = jnp.uint32

GRID_W = 64
LN_EPS = 1e-5
HEAD_DIM = 64
V_DIM = 2 * HEAD_DIM
ATTN_SCALE = HEAD_DIM ** -0.5
Q_SCALE_LOG2 = ATTN_SCALE * math.log2(math.e)
ROPE_BASE = 10000.0
ROPE_AXIS_PAIRS = HEAD_DIM // 4
CHUNK = 128
SGU_GROUPS = 4
POOL_WINDOWS = (2, 4, 8, 16)
P_HEADS = 8
N_KEYS = 128
P_TOPK = 16

LANE = 128
SUBLANE = 8
BF16_ROWS = 16
VMEM_LIMIT = 56 * 1024 * 1024
TOKEN_TILE = 1024


def _params(*sem):
    return pltpu.CompilerParams(dimension_semantics=sem, vmem_limit_bytes=VMEM_LIMIT)


def _gelu_tanh(x):
    c1 = math.sqrt(2.0 / math.pi)
    c2 = c1 * 0.044715
    inner = x * (c1 + c2 * (x * x))
    return x * (0.5 + 0.5 * jnp.tanh(inner))


def _mm_kernel(*refs, n_pairs, has_bias):
    o_ref = refs[-1]
    acc = None
    for i in range(n_pairs):
        part = jnp.dot(refs[2 * i][...].astype(BF16), refs[2 * i + 1][...].astype(BF16),
                       preferred_element_type=F32)
        acc = part if acc is None else acc + part
    if has_bias:
        acc = acc + refs[2 * n_pairs][...]
    o_ref[...] = acc.astype(o_ref.dtype)


def matmul_fm(pairs, out_dtype, bias=None, tn=TOKEN_TILE, tm=None):
    m = pairs[0][0].shape[0]
    n = pairs[0][1].shape[1]
    tn = min(tn, n)
    tm = m if tm is None else tm
    args, specs = [], []
    for w, x in pairs:
        k = w.shape[1]
        args += [w, x]
        specs += [pl.BlockSpec((tm, k), lambda j, i: (i, 0)),
                  pl.BlockSpec((k, tn), lambda j, i: (0, j))]
    if bias is not None:
        args.append(bias)
        specs.append(pl.BlockSpec((tm, 1), lambda j, i: (i, 0)))
    return pl.pallas_call(
        functools.partial(_mm_kernel, n_pairs=len(pairs), has_bias=bias is not None),
        grid=(n // tn, m // tm),
        in_specs=specs,
        out_specs=pl.BlockSpec((tm, tn), lambda j, i: (i, j)),
        out_shape=jax.ShapeDtypeStruct((m, n), out_dtype),
        compiler_params=_params("parallel", "parallel"),
        name="matmul_fm",
    )(*args)


def _ln_mod_kernel(*refs, alpha, has_resid, emit_x, emit_h):
    it = iter(refs)
    x = next(it)[...]
    if has_resid:
        y_ref, gate_ref, lng_ref, lnb_ref = next(it), next(it), next(it), next(it)
    if emit_h:
        shift_ref, scale_ref = next(it), next(it)
    if has_resid:
        z = alpha * x + gate_ref[...] * y_ref[...]
        mu = jnp.mean(z, axis=0, keepdims=True)
        zc = z - mu
        var = jnp.mean(zc * zc, axis=0, keepdims=True)
        x = zc * lax.rsqrt(var + LN_EPS) * lng_ref[...] + lnb_ref[...]
        if emit_x:
            next(it)[...] = x
    if emit_h:
        next(it)[...] = (x * (1.0 + scale_ref[...]) + shift_ref[...]).astype(BF16)


def ln_mod_fm(xs, *, resid=None, mod=None, alpha=1.0, emit_x=True, tn=TOKEN_TILE):
    d, n = xs.shape
    tn = min(tn, n)
    n_seg = (resid[1] if resid is not None else mod[0]).shape[1]
    tps = (n // n_seg) // tn
    tok = pl.BlockSpec((d, tn), lambda j: (0, j))
    seg = pl.BlockSpec((None, d, 1), lambda j: (j // tps, 0, 0))
    col = pl.BlockSpec((d, 1), lambda j: (0, 0))
    args, specs, out_shape, out_specs = [xs], [tok], [], []
    if resid is not None:
        y, gate, ln_g, ln_b = resid
        args += [y, gate.T[:, :, None], ln_g[:, None], ln_b[:, None]]
        specs += [tok, seg, col, col]
        if emit_x:
            out_shape.append(jax.ShapeDtypeStruct((d, n), F32))
            out_specs.append(tok)
    if mod is not None:
        args += [mod[0].T[:, :, None], mod[1].T[:, :, None]]
        specs += [seg, seg]
        out_shape.append(jax.ShapeDtypeStruct((d, n), BF16))
        out_specs.append(tok)
    outs = pl.pallas_call(
        functools.partial(_ln_mod_kernel, alpha=alpha, has_resid=resid is not None,
                          emit_x=emit_x and resid is not None, emit_h=mod is not None),
        grid=(n // tn,),
        in_specs=specs,
        out_specs=out_specs,
        out_shape=out_shape,
        compiler_params=_params("parallel"),
        name="ln_mod",
    )(*args)
    outs = list(outs)
    x_new = outs.pop(0) if (resid is not None and emit_x) else None
    h = outs.pop(0) if mod is not None else None
    return x_new, h


ATTN_Q_TILE = 1024
V_ROWS = V_DIM + BF16_ROWS


def _attn_kernel(lam_ref, q_ref, k_ref, v_ref, ones_ref, g_ref, o_ref, qp_ref, m_ref, acc_ref,
                 *, tq, tk, n_chunks, out_scale):
    q = q_ref[...]
    row = lax.broadcasted_iota(jnp.int32, q.shape, 0)
    zero = jnp.zeros_like(q)
    qp_ref[:, :tq] = jnp.where(row < HEAD_DIM, q, zero)
    qp_ref[:, tq:] = jnp.where(row >= HEAD_DIM, q, zero)
    m_ref[...] = jnp.full(m_ref.shape, -jnp.inf, F32)
    acc_ref[...] = jnp.zeros(acc_ref.shape, F32)

    n_groups = 2
    items = [(c, g) for c in range(n_chunks) for g in range(n_groups)]

    def lanes_of(item):
        return slice(item[1] * tq, (item[1] + 1) * tq)

    def scores(item):
        return jnp.dot(k_ref[item[0] * tk:(item[0] + 1) * tk, :], qp_ref[:, lanes_of(item)],
                       preferred_element_type=F32)

    def running_max(item, s):
        m_prev = m_ref[:, lanes_of(item)]
        m_new = jnp.maximum(m_prev, jnp.max(s, axis=0, keepdims=True))
        m_ref[:, lanes_of(item)] = m_new
        return m_new, jnp.exp2(m_prev - m_new)

    assert n_groups >= 2
    n_items = len(items)
    s_buf = {0: scores(items[0])}
    if n_items > 1:
        s_buf[1] = scores(items[1])
    stats = {0: running_max(items[0], s_buf[0])}
    for idx, item in enumerate(items):
        if idx + 2 < n_items:
            s_buf[idx + 2] = scores(items[idx + 2])
        if idx + 1 < n_items:
            stats[idx + 1] = running_max(items[idx + 1], s_buf[idx + 1])
        m_new, alpha = stats.pop(idx)
        p = jnp.exp2(s_buf.pop(idx) - m_new).astype(BF16)
        lanes = lanes_of(item)
        v_ext = jnp.concatenate([v_ref[:, item[0] * tk:(item[0] + 1) * tk], ones_ref[...]], axis=0)
        acc_ref[:, lanes] = acc_ref[:, lanes] * alpha + jnp.dot(v_ext, p, preferred_element_type=F32)

    lam = lam_ref[0, 0]
    acc = acc_ref[...]
    inv = 1.0 / acc[V_DIM:V_DIM + 1, :]
    o = acc[:V_DIM, :tq] * inv[:, :tq] - lam * (acc[:V_DIM, tq:] * inv[:, tq:])
    ms = jnp.mean(o * o, axis=0, keepdims=True)
    o = o * lax.rsqrt(ms + LN_EPS) * g_ref[...] * out_scale
    o_ref[...] = o.astype(o_ref.dtype)


def diff_attention(q_fm, k_tok, v_fm, lam, norm_g, lam_init, n_batch, tq, tk):
    n_heads = q_fm.shape[0] // V_DIM
    sq = q_fm.shape[1] // n_batch
    nq = sq // tq
    l_keys = k_tok.shape[0] // n_batch
    nc = l_keys // tk
    ones = jnp.zeros((BF16_ROWS, tk), BF16).at[0].set(1.0)
    return pl.pallas_call(
        functools.partial(_attn_kernel, tq=tq, tk=tk, n_chunks=nc, out_scale=1.0 - lam_init),
        grid=(n_batch, n_heads, nq),
        in_specs=[
            pl.BlockSpec(memory_space=pltpu.SMEM),
            pl.BlockSpec((V_DIM, tq), lambda b, h, i: (h, b * nq + i)),
            pl.BlockSpec((l_keys, V_DIM), lambda b, h, i: (b, h)),
            pl.BlockSpec((V_DIM, l_keys), lambda b, h, i: (h, b)),
            pl.BlockSpec((BF16_ROWS, tk), lambda b, h, i: (0, 0)),
            pl.BlockSpec((V_DIM, 1), lambda b, h, i: (0, 0)),
        ],
        out_specs=pl.BlockSpec((V_DIM, tq), lambda b, h, i: (h, b * nq + i)),
        out_shape=jax.ShapeDtypeStruct(q_fm.shape, BF16),
        scratch_shapes=[
            pltpu.VMEM((V_DIM, 2 * tq), BF16),
            pltpu.VMEM((1, 2 * tq), F32),
            pltpu.VMEM((V_ROWS, 2 * tq), F32),
        ],
        compiler_params=_params("parallel", "parallel", "parallel"),
        name="diff_attention",
    )(lam, q_fm, k_tok, v_fm, ones, norm_g)


def _ab_in_kernel(h_ref, w_ref, cos_ref, sin_ref, lng_ref, lnb_ref, wst_ref, bs_ref,
                  q_ref, k_ref, v_ref, g_ref, *, qk_w, sgu_half):
    p = jnp.dot(w_ref[...], h_ref[...], preferred_element_type=F32)
    cs, sn = cos_ref[...], sin_ref[...]
    half = HEAD_DIM // 2
    k_rows = []
    for blk in range(qk_w // HEAD_DIM):
        r = blk * HEAD_DIM
        a, b = p[r:r + half], p[r + half:r + HEAD_DIM]
        q_ref[r:r + half, :] = ((a * cs - b * sn) * Q_SCALE_LOG2).astype(BF16)
        q_ref[r + half:r + HEAD_DIM, :] = ((a * sn + b * cs) * Q_SCALE_LOG2).astype(BF16)
        a, b = p[qk_w + r:qk_w + r + half], p[qk_w + r + half:qk_w + r + HEAD_DIM]
        k_rows += [a * cs - b * sn, a * sn + b * cs]
    k_ref[...] = jnp.concatenate(k_rows, axis=0).T.astype(BF16)
    v_ref[...] = p[2 * qk_w:3 * qk_w].astype(BF16)

    gu = p[3 * qk_w:3 * qk_w + sgu_half]
    v = _gelu_tanh(p[3 * qk_w + sgu_half:])
    mu = jnp.mean(v, axis=0, keepdims=True)
    vc = v - mu
    var = jnp.mean(vc * vc, axis=0, keepdims=True)
    vb = (vc * lax.rsqrt(var + LN_EPS) * lng_ref[...] + lnb_ref[...]).astype(BF16)
    gc = sgu_half // SGU_GROUPS
    for g in range(SGU_GROUPS):
        for j in range(vb.shape[1] // CHUNK):
            rows, cols = slice(g * gc, (g + 1) * gc), slice(j * CHUNK, (j + 1) * CHUNK)
            s = jnp.dot(vb[rows, cols], wst_ref[g], preferred_element_type=F32) + bs_ref[g]
            g_ref[rows, cols] = (_gelu_tanh(gu[rows, cols]) * s).astype(BF16)


def ab_in_proj(h_bf, w_in_t, cos_tab, sin_tab, ln_g, ln_b, ws_t, bs, tn=TOKEN_TILE):
    d, n = h_bf.shape
    tn = min(tn, n)
    qk_w = w_in_t.shape[0] // 4
    sgu_half = (w_in_t.shape[0] - 3 * qk_w) // 2
    n_pos = cos_tab.shape[1] // tn
    tok = lambda rows: pl.BlockSpec((rows, tn), lambda j: (0, j))
    pos = pl.BlockSpec((HEAD_DIM // 2, tn), lambda j: (0, j % n_pos))
    return pl.pallas_call(
        functools.partial(_ab_in_kernel, qk_w=qk_w, sgu_half=sgu_half),
        grid=(n // tn,),
        in_specs=[
            tok(d),
            pl.BlockSpec(w_in_t.shape, lambda j: (0, 0)),
            pos, pos,
            pl.BlockSpec((sgu_half, 1), lambda j: (0, 0)),
            pl.BlockSpec((sgu_half, 1), lambda j: (0, 0)),
            pl.BlockSpec((SGU_GROUPS, CHUNK, CHUNK), lambda j: (0, 0, 0)),
            pl.BlockSpec((SGU_GROUPS, 1, CHUNK), lambda j: (0, 0, 0)),
        ],
        out_specs=[tok(qk_w), pl.BlockSpec((tn, qk_w), lambda j: (j, 0)), tok(qk_w), tok(sgu_half)],
        out_shape=[jax.ShapeDtypeStruct((qk_w, n), BF16), jax.ShapeDtypeStruct((n, qk_w), BF16),
                   jax.ShapeDtypeStruct((qk_w, n), BF16), jax.ShapeDtypeStruct((sgu_half, n), BF16)],
        compiler_params=_params("parallel"),
        name="ab_in_proj",
    )(h_bf, w_in_t, cos_tab, sin_tab, ln_g, ln_b, ws_t, bs)


POOL_TILE = 512
POOL_HALO = 128


def _pool_kernel(left_ref, mid_ref, right_ref, wg_ref, scale_ref, wout_ref, o_ref, *, seq_len, tile):
    j = pl.program_id(0)
    t0 = j * tile
    seq_start = (t0 // seq_len) * seq_len
    win = tile + 2 * POOL_HALO
    h_all = jnp.concatenate([left_ref[...], mid_ref[...], right_ref[...]], axis=1)
    h_hi = h_all.astype(BF16)
    h_lo = (h_all - h_hi.astype(F32)).astype(BF16)
    tau = t0 - POOL_HALO + lax.broadcasted_iota(jnp.int32, (win, tile), 0)
    t = t0 + lax.broadcasted_iota(jnp.int32, (win, tile), 1)
    gc = mid_ref.shape[0] // len(POOL_WINDOWS)
    outs = []
    for g, w in enumerate(POOL_WINDOWS):
        lo = jnp.maximum(t - w // 2, seq_start)
        hi = jnp.minimum(t + (w - w // 2), seq_start + seq_len)
        band = jnp.where((tau >= lo) & (tau < hi), 1.0, 0.0).astype(BF16)
        cnt = (hi - lo)[0:1, :].astype(F32)
        sl = slice(g * gc, (g + 1) * gc)
        ssum = (jnp.dot(h_hi[sl], band, preferred_element_type=F32)
                + jnp.dot(h_lo[sl], band, preferred_element_type=F32))
        m = ssum / cnt - mid_ref[sl, :]
        m = jnp.dot(wg_ref[g], m.astype(BF16), preferred_element_type=F32)
        outs.append(m)
    m_all = (jnp.concatenate(outs, axis=0) * scale_ref[...]).astype(BF16)
    o_ref[...] = jnp.dot(wout_ref[...], m_all, preferred_element_type=F32).astype(o_ref.dtype)


def pool_mixer_fm(h, wg_t, scale, wout_t, seq_len):
    c, n = h.shape
    tile = min(POOL_TILE, seq_len)
    nt = n // tile
    r = tile // POOL_HALO
    nh = n // POOL_HALO
    return pl.pallas_call(
        functools.partial(_pool_kernel, seq_len=seq_len, tile=tile),
        grid=(nt,),
        in_specs=[
            pl.BlockSpec((c, POOL_HALO), lambda j: (0, jnp.maximum(j * r - 1, 0))),
            pl.BlockSpec((c, tile), lambda j: (0, j)),
            pl.BlockSpec((c, POOL_HALO), lambda j: (0, jnp.minimum(j * r + r, nh - 1))),
            pl.BlockSpec(wg_t.shape, lambda j: (0, 0, 0)),
            pl.BlockSpec((c, 1), lambda j: (0, 0)),
            pl.BlockSpec(wout_t.shape, lambda j: (0, 0)),
        ],
        out_specs=pl.BlockSpec((wout_t.shape[0], tile), lambda j: (0, j)),
        out_shape=jax.ShapeDtypeStruct((wout_t.shape[0], n), BF16),
        compiler_params=_params("parallel"),
        name="pool_mixer",
    )(h, h, h, wg_t, scale, wout_t)


def _sort16_desc(xs):
    xs = list(xs)
    n = len(xs)
    k = 2
    while k <= n:
        j = k // 2
        while j >= 1:
            for i in range(n):
                l = i ^ j
                if l > i:
                    hi, lo = jnp.maximum(xs[i], xs[l]), jnp.minimum(xs[i], xs[l])
                    if (i & k) == 0:
                        xs[i], xs[l] = hi, lo
                    else:
                        xs[i], xs[l] = lo, hi
            j //= 2
        k *= 2
    return xs


def _bitonic_merge_desc(xs):
    xs = list(xs)
    n = len(xs)
    j = n // 2
    while j >= 1:
        for i in range(n):
            l = i ^ j
            if l > i:
                xs[i], xs[l] = jnp.maximum(xs[i], xs[l]), jnp.minimum(xs[i], xs[l])
        j //= 2
    return xs


def _top16_of_two(a, b):
    n = len(a)
    return _bitonic_merge_desc([jnp.maximum(a[i], b[n - 1 - i]) for i in range(n)])


def _top16_desc(vals):
    groups = [_sort16_desc(vals[i:i + P_TOPK]) for i in range(0, len(vals), P_TOPK)]
    while len(groups) > 1:
        groups = [_top16_of_two(groups[i], groups[i + 1]) for i in range(0, len(groups), 2)]
    return groups[0]


def _peer_route_kernel(h_ref, wq_ref, keys_ref, lrow_ref, e1_ref, rank2_ref, e2_ref, s_s):
    q = jnp.dot(wq_ref[...], h_ref[...], preferred_element_type=F32).astype(BF16)
    n_half = h_ref.shape[1] // LANE

    def sheet(p, rows):
        return jnp.concatenate([s_s[half, p, rows, :] for half in range(n_half)], axis=1)

    for hp in range(2 * P_HEADS):
        h, p = divmod(hp, 2)
        s = jnp.dot(keys_ref[hp], q[hp * N_KEYS:(hp + 1) * N_KEYS, :], preferred_element_type=F32)
        for half in range(n_half):
            s_s[half, p, pl.ds(h, N_KEYS, stride=P_HEADS), :] = s[:, half * LANE:(half + 1) * LANE]

    s1 = [sheet(0, slice(k * P_HEADS, (k + 1) * P_HEADS)) for k in range(N_KEYS)]
    s2 = [sheet(1, slice(k * P_HEADS, (k + 1) * P_HEADS)) for k in range(N_KEYS)]
    a = _top16_desc(s1)
    b = _top16_desc(s2)
    cand = [[a[r1] + b[r2] for r2 in range(P_TOPK)] for r1 in range(P_TOPK)]
    rows = [list(r) for r in cand]
    while len(rows) > 1:
        rows = [_top16_of_two(rows[i], rows[i + 1]) for i in range(0, len(rows), 2)]
    best = rows[0]
    thr = best[P_TOPK - 1]
    z = None
    for r in range(P_TOPK):
        e = jnp.exp(best[r] - best[0])
        z = e if z is None else z + e
    inv_z = 1.0 / z
    counts = []
    for r1 in range(P_TOPK):
        cnt = jnp.zeros_like(thr)
        for r2 in range(P_TOPK):
            cnt = cnt + jnp.where(cand[r1][r2] >= thr, 1.0, 0.0)
        counts.append(cnt)

    def per_key(k, carry):
        base = pl.multiple_of(k * P_HEADS, P_HEADS)
        s1k = sheet(0, pl.ds(base, P_HEADS))
        s2k = sheet(1, pl.ds(base, P_HEADS))
        lrow = jnp.zeros_like(thr)
        rank = jnp.zeros_like(thr)
        for r in range(P_TOPK):
            lrow = jnp.where(s1k == a[r], counts[r], lrow)
            rank = jnp.where(b[r] > s2k, float(r + 1), rank)
        lrow_ref[k] = lrow
        e1_ref[k] = jnp.exp(s1k - a[0])
        rank2_ref[k] = rank
        e2_ref[k] = jnp.exp(s2k - b[0]) * inv_z
        return carry

    lax.fori_loop(0, N_KEYS, per_key, 0, unroll=8)


def peer_route(h_bf, wq_t, keys_bf, tn=2 * LANE):
    d, n = h_bf.shape
    rows = N_KEYS * P_HEADS
    spec = pl.BlockSpec((N_KEYS, P_HEADS, tn), lambda j: (0, 0, j))
    sheet = lambda dt: jax.ShapeDtypeStruct((N_KEYS, P_HEADS, n), dt)
    return pl.pallas_call(
        _peer_route_kernel,
        grid=(n // tn,),
        in_specs=[
            pl.BlockSpec((d, tn), lambda j: (0, j)),
            pl.BlockSpec(wq_t.shape, lambda j: (0, 0)),
            pl.BlockSpec(keys_bf.shape, lambda j: (0, 0, 0)),
        ],
        out_specs=[spec] * 4,
        out_shape=[sheet(F32)] * 4,
        scratch_shapes=[pltpu.VMEM((tn // LANE, 2, rows, LANE), F32)],
        compiler_params=_params("parallel"),
        name="peer_route",
    )(h_bf, wq_t, keys_bf)


PEER_TOKEN_TILE = 1024
PEER_I1_TILE = 8


def _peer_dense_kernel(h_ref, u_ref, vt_ref, lrow_ref, e1_ref, rank2_ref, e2_ref, o_ref, w_s, acc_s, *, i1_tile):
    e = pl.program_id(1)

    @pl.when(e == 0)
    def _():
        acc_s[...] = jnp.zeros(acc_s.shape, F32)

    act = _gelu_tanh(jnp.dot(u_ref[...], h_ref[...], preferred_element_type=F32).astype(BF16))
    tn = act.shape[1]
    zero = jnp.zeros((), BF16)
    for il in range(i1_tile):
        lr = lrow_ref[il * P_HEADS:(il + 1) * P_HEADS, :]
        e1 = e1_ref[il * P_HEADS:(il + 1) * P_HEADS, :]
        gate = None
        for h in range(P_HEADS):
            lb = jnp.broadcast_to(lr[h:h + 1, :], (N_KEYS, tn)).astype(BF16)
            eb = jnp.broadcast_to(e1[h:h + 1, :], (N_KEYS, tn)).astype(BF16)
            keys = slice(h * N_KEYS, (h + 1) * N_KEYS)
            term = jnp.where(rank2_ref[keys, :] < lb, e2_ref[keys, :], zero) * eb
            gate = term if gate is None else gate + term
        w_s[il * N_KEYS:(il + 1) * N_KEYS, :] = gate * act[il * N_KEYS:(il + 1) * N_KEYS, :]
    acc_s[...] += jnp.dot(vt_ref[...], w_s[...], preferred_element_type=F32)

    @pl.when(e == pl.num_programs(1) - 1)
    def _():
        o_ref[...] = acc_s[...].astype(o_ref.dtype)


def peer_dense(h_bf, u_bf, vt_bf, layer, lrow, e1, rank2, e2, tn=PEER_TOKEN_TILE, i1_tile=PEER_I1_TILE):
    d, n = h_bf.shape
    n_exp = u_bf.shape[1]
    tn = min(tn, n)
    et = i1_tile * N_KEYS
    table = pl.BlockSpec((P_HEADS * N_KEYS, tn), lambda j, e: (0, j))
    return pl.pallas_call(
        functools.partial(_peer_dense_kernel, i1_tile=i1_tile),
        grid=(n // tn, n_exp // et),
        in_specs=[
            pl.BlockSpec((d, tn), lambda j, e: (0, j)),
            pl.BlockSpec((None, et, d), lambda j, e: (layer, e, 0)),
            pl.BlockSpec((None, d, et), lambda j, e: (layer, 0, e)),
            pl.BlockSpec((i1_tile * P_HEADS, tn), lambda j, e: (e, j)),
            pl.BlockSpec((i1_tile * P_HEADS, tn), lambda j, e: (e, j)),
            table, table,
        ],
        out_specs=pl.BlockSpec((d, tn), lambda j, e: (0, j)),
        out_shape=jax.ShapeDtypeStruct((d, n), BF16),
        scratch_shapes=[pltpu.VMEM((et, tn), BF16), pltpu.VMEM((d, tn), F32)],
        compiler_params=_params("parallel", "arbitrary"),
        name="peer_dense",
    )(h_bf, u_bf, vt_bf, lrow, e1, rank2, e2)


def peer_fm(h_bf, wq_t, keys_bf, u_bf, vt_bf, layer):
    lrow, e1, rank2, e2 = peer_route(h_bf, wq_t, keys_bf)
    rows, n = N_KEYS * P_HEADS, h_bf.shape[1]
    head_major = lambda t: jnp.transpose(t, (1, 0, 2)).reshape(rows, n).astype(BF16)
    return peer_dense(h_bf, u_bf, vt_bf, layer, lrow.reshape(rows, n), e1.reshape(rows, n),
                      head_major(rank2), head_major(e2))


def _rope_tables_fm(rows):
    row = jnp.repeat(jnp.arange(rows, dtype=F32), GRID_W)
    col = jnp.tile(jnp.arange(GRID_W, dtype=F32), rows)
    inv = ROPE_BASE ** (-jnp.arange(ROPE_AXIS_PAIRS, dtype=F32) / ROPE_AXIS_PAIRS)
    ang = jnp.concatenate([inv[:, None] * row[None, :], inv[:, None] * col[None, :]], axis=0)
    return jnp.cos(ang), jnp.sin(ang)


def _kv_all(kc_tok, kx_tok, vc_fm, vx_fm, n_batch):
    c_len, s_len = kc_tok.shape[0] // n_batch, kx_tok.shape[0] // n_batch
    k_parts, v_parts = [], []
    for b in range(n_batch):
        k_parts += [kc_tok[b * c_len:(b + 1) * c_len], kx_tok[b * s_len:(b + 1) * s_len]]
        v_parts += [vc_fm[:, b * c_len:(b + 1) * c_len], vx_fm[:, b * s_len:(b + 1) * s_len]]
    return jnp.concatenate(k_parts, axis=0), jnp.concatenate(v_parts, axis=1)


def _pick_tk(l):
    for tk in (768, 512, 256, 128):
        if l % tk == 0:
            return tk
    raise ValueError(f"unsupported key length {l}")


def kernel(x, c, ctx, c_ctx, ada_w, ada_b, ln_g, ln_b, ab_w_in, ab_w_out, diff_lam, diff_norm_g,
           sgu_ln_g, sgu_ln_b, sgu_w, sgu_b, pool_w_in, pool_w_grp, pool_scale, pool_w_out,
           peer_wq, peer_keys, peer_u, peer_v):
    bsz, s_len, d = x.shape
    c_len = ctx.shape[1]
    depth = ada_w.shape[0]
    qk_w = ab_w_in.shape[2] // 4
    alpha = (2.0 * depth) ** 0.25
    last_ctx_read = 2 * ((depth - 1) // 2)
    cos, sin = _rope_tables_fm(s_len // GRID_W)
    c_tile = min(TOKEN_TILE, bsz * c_len)
    cos_c = jnp.ones((HEAD_DIM // 2, c_tile), F32)
    sin_c = jnp.zeros((HEAD_DIM // 2, c_tile), F32)

    xs = x.reshape(bsz * s_len, d).T
    cs = ctx.reshape(bsz * c_len, d).T

    cond = jnp.zeros((d, LANE), F32)
    cond = cond.at[:, :bsz].set(jax.nn.silu(c).T).at[:, bsz].set(jax.nn.silu(c_ctx))
    cond_t = cond.T.astype(BF16)
    u_bf = peer_u.astype(BF16)
    vt_bf = jnp.transpose(peer_v, (0, 2, 1)).astype(BF16)
    mods_x, mods_c = [], []
    for i in range(depth):
        ada = matmul_fm([(cond_t, ada_w[i])], F32).T + ada_b[i][:, None]
        mods_x.append([ada[k * d:(k + 1) * d, :bsz] for k in range(6)])
        mods_c.append([ada[k * d:(k + 1) * d, bsz:bsz + 1] for k in range(6)])

    def uses_ctx(i):
        return i < depth and (i < last_ctx_read or i % 2 == 0)

    _, hx = ln_mod_fm(xs, mod=(mods_x[0][0], mods_x[0][1]))
    hc = ln_mod_fm(cs, mod=(mods_c[0][0], mods_c[0][1]))[1] if uses_ctx(0) else None

    for i in range(depth):
        j = i // 2
        even = i % 2 == 0
        ctx_out = i < last_ctx_read
        last = i == depth - 1
        sh1, sc1, g1, sh2, sc2, g2 = mods_x[i]
        csh1, csc1, cg1, csh2, csc2, cg2 = mods_c[i]

        if even:
            lam_init = 0.8 - 0.6 * math.exp(-0.3 * i)
            lq1, lk1, lq2, lk2 = diff_lam[j].astype(F32)
            lam = (jnp.exp(jnp.sum(lq1 * lk1)) - jnp.exp(jnp.sum(lq2 * lk2)) + lam_init).reshape(1, 1)
            w_in_t = ab_w_in[j].T.astype(BF16)
            w_out_t = ab_w_out[j].T.astype(BF16)
            norm_g = diff_norm_g[j][:, None]
            sgu_args = (sgu_ln_g[j][:, None], sgu_ln_b[j][:, None],
                        jnp.transpose(sgu_w[j], (0, 2, 1)).astype(BF16), sgu_b[j][:, None, :])
            q_x, k_x, v_x, g_x = ab_in_proj(hx, w_in_t, cos, sin, *sgu_args)
            q_c, k_c, v_c, g_c = ab_in_proj(hc, w_in_t, cos_c, sin_c, *sgu_args)
            k_all, v_all = _kv_all(k_c, k_x, v_c, v_x, bsz)
            a_x = diff_attention(q_x, k_all, v_all, lam, norm_g, lam_init, bsz,
                                 tq=min(ATTN_Q_TILE, s_len), tk=_pick_tk(c_len + s_len))
            yx = matmul_fm([(w_out_t[:, :qk_w], a_x), (w_out_t[:, qk_w:], g_x)], BF16)
            if ctx_out:
                a_c = diff_attention(q_c, k_c, v_c, lam, norm_g, lam_init, bsz, tq=c_len, tk=_pick_tk(c_len))
                yc = matmul_fm([(w_out_t[:, :qk_w], a_c), (w_out_t[:, qk_w:], g_c)], BF16)
        else:
            p_in_t = pool_w_in[j].T.astype(BF16)
            wg_t = jnp.transpose(pool_w_grp[j], (0, 2, 1)).astype(BF16)
            p_out_t = pool_w_out[j].T.astype(BF16)
            scale = pool_scale[j][:, None]
            yx = pool_mixer_fm(matmul_fm([(p_in_t, hx)], F32), wg_t, scale, p_out_t, s_len)
            if ctx_out:
                yc = pool_mixer_fm(matmul_fm([(p_in_t, hc)], F32), wg_t, scale, p_out_t, c_len)

        wq_t = peer_wq[i].T.astype(BF16)
        keys_bf = peer_keys[i].reshape(2 * P_HEADS, N_KEYS, -1).astype(BF16)
        next_mod_c = (mods_c[i + 1][0], mods_c[i + 1][1]) if uses_ctx(i + 1) else None
        if ctx_out:
            cs, hc = ln_mod_fm(cs, resid=(yc, cg1, ln_g[i, 0], ln_b[i, 0]), mod=(csh2, csc2), alpha=alpha)
            fc = peer_fm(hc, wq_t, keys_bf, u_bf, vt_bf, i)
            if uses_ctx(i + 1):
                cs, hc = ln_mod_fm(cs, resid=(fc, cg2, ln_g[i, 1], ln_b[i, 1]), mod=next_mod_c, alpha=alpha)
        elif next_mod_c is not None:
            hc = ln_mod_fm(cs, mod=next_mod_c)[1]

        xs, hx = ln_mod_fm(xs, resid=(yx, g1, ln_g[i, 0], ln_b[i, 0]), mod=(sh2, sc2), alpha=alpha)
        fx = peer_fm(hx, wq_t, keys_bf, u_bf, vt_bf, i)
        next_mod_x = None if last else (mods_x[i + 1][0], mods_x[i + 1][1])
        xs, hx = ln_mod_fm(xs, resid=(fx, g2, ln_g[i, 1], ln_b[i, 1]), mod=next_mod_x, alpha=alpha)

    return xs.T.reshape(bsz, s_len, d)
```
